```python
import math
import jax, jax.numpy as jnp
from jax import lax
import numpy as np

D_MODEL = 1024
BATCH = 2
SEQ = 8192
DEPTH = 1

CHUNK = 64
QBLOCK = 128
MIX_WIDTH = D_MODEL
DIFF_HEADS = 4
DIFF_DH = 64
DIFF_VD = 2 * DIFF_DH
FOX_HEADS = 8
FOX_DH = 64
ROPE_THETA = 500000.0
ROPE_DIM = DIFF_DH // 4
D_FF = ((8 * D_MODEL // 3 + 255) // 256) * 256
EPS = 1e-5
NEG = -1e30

DIFF_QK_W = DIFF_HEADS * 2 * DIFF_DH
DIFF_V_W = DIFF_HEADS * DIFF_VD
FOX_W = FOX_HEADS * FOX_DH
PROJ_W = 2 * DIFF_QK_W + DIFF_V_W + 3 * FOX_W + FOX_HEADS
SPLITS = (DIFF_QK_W,
          2 * DIFF_QK_W,
          2 * DIFF_QK_W + DIFF_V_W,
          2 * DIFF_QK_W + DIFF_V_W + FOX_W,
          2 * DIFF_QK_W + DIFF_V_W + 2 * FOX_W,
          2 * DIFF_QK_W + DIFF_V_W + 3 * FOX_W)

kernel_name = "hybrid_diffattn_forgetting_attn_block"


def rmsnorm(x, g):
    xf = x.astype(jnp.float32)
    y = xf * lax.rsqrt(jnp.mean(xf * xf, axis=-1, keepdims=True) + EPS)
    return (y * g.astype(jnp.float32)).astype(x.dtype)


def rope_partial(t, cos, sin):
    half = ROPE_DIM // 2
    shp = (1, t.shape[1]) + (1,) * (t.ndim - 3) + (half,)
    c = cos.reshape(shp).astype(t.dtype)
    s = sin.reshape(shp).astype(t.dtype)
    t1 = t[..., :half]
    t2 = t[..., half:ROPE_DIM]
    return jnp.concatenate([t1 * c - t2 * s, t2 * c + t1 * s, t[..., ROPE_DIM:]], axis=-1)


def to_blocks(t, nb):
    b, _, h, d = t.shape
    return t.reshape(b, nb, QBLOCK, h, d).transpose(1, 0, 3, 2, 4)


def from_blocks(t):
    nb, b, h, q, d = t.shape
    return t.transpose(1, 0, 3, 2, 4).reshape(b, nb * q, h * d)


def setup_inputs(seed: int = 0) -> dict:
    key = jax.random.key(seed)
    ks = jax.random.split(key, 16)
    f32 = jnp.float32
    nrm = lambda k, shp, s: jax.random.normal(k, shp, f32) * s
    return {
        "x": jax.random.normal(ks[0], (BATCH, SEQ, D_MODEL), f32),
        "norm1_g": 1.0 + nrm(ks[1], (DEPTH, D_MODEL), 0.02),
        "w_in": nrm(ks[2], (DEPTH, D_MODEL, PROJ_W), D_MODEL ** -0.5),
        "b_f": 3.0 + nrm(ks[3], (DEPTH, FOX_HEADS), 0.5),
        "lam_q1": nrm(ks[4], (DEPTH, DIFF_DH), 0.1),
        "lam_k1": nrm(ks[5], (DEPTH, DIFF_DH), 0.1),
        "lam_q2": nrm(ks[6], (DEPTH, DIFF_DH), 0.1),
        "lam_k2": nrm(ks[7], (DEPTH, DIFF_DH), 0.1),
        "subln_g": 1.0 + nrm(ks[8], (DEPTH, DIFF_VD), 0.02),
        "w_out": nrm(ks[9], (DEPTH, MIX_WIDTH, D_MODEL), MIX_WIDTH ** -0.5),
        "norm2_g": 1.0 + nrm(ks[10], (DEPTH, D_MODEL), 0.02),
        "w_gate": nrm(ks[11], (DEPTH, D_MODEL, D_FF), D_MODEL ** -0.5),
        "w_up": nrm(ks[12], (DEPTH, D_MODEL, D_FF), D_MODEL ** -0.5),
        "w_down": nrm(ks[13], (DEPTH, D_FF, D_MODEL), D_FF ** -0.5),
        "normf_g": 1.0 + nrm(ks[14], (D_MODEL,), 0.02),
    }


def reference(x, norm1_g, w_in, b_f, lam_q1, lam_k1, lam_q2, lam_k2, subln_g, w_out,
              norm2_g, w_gate, w_up, w_down, normf_g):
    f32 = jnp.float32
    B, S, _ = x.shape
    nb = S // QBLOCK
    pos = jnp.arange(S, dtype=jnp.int32)
    inv_freq = ROPE_THETA ** (-jnp.arange(0, ROPE_DIM, 2, dtype=f32) / ROPE_DIM)
    ang = pos.astype(f32)[:, None] * inv_freq[None, :]
    cos, sin = jnp.cos(ang), jnp.sin(ang)
    k_chunk = pos // CHUNK
    scale_d = DIFF_DH ** -0.5
    scale_f = FOX_DH ** -0.5

    for l in range(DEPTH):
        lam_init = 0.8 - 0.6 * math.exp(-0.3 * l)
        h = rmsnorm(x, norm1_g[l])
        proj = jnp.einsum('bsd,de->bse', h, w_in[l])
        dq, dk, dv, fq, fk, fv, fg = jnp.split(proj, SPLITS, axis=-1)

        dq = rope_partial(dq.reshape(B, S, DIFF_HEADS, 2, DIFF_DH), cos, sin)
        dk = rope_partial(dk.reshape(B, S, DIFF_HEADS, 2, DIFF_DH), cos, sin)
        dv_t = dv.reshape(B, S, DIFF_HEADS, DIFF_VD).transpose(0, 2, 1, 3)
        k1 = dk[:, :, :, 0].transpose(0, 2, 1, 3)
        k2 = dk[:, :, :, 1].transpose(0, 2, 1, 3)
        lam = (jnp.exp(jnp.sum(lam_q1[l].astype(f32) * lam_k1[l].astype(f32)))
               - jnp.exp(jnp.sum(lam_q2[l].astype(f32) * lam_k2[l].astype(f32)))
               + lam_init)

        fq = fq.reshape(B, S, FOX_HEADS, FOX_DH)
        fk_t = fk.reshape(B, S, FOX_HEADS, FOX_DH).transpose(0, 2, 1, 3)
        fv_t = fv.reshape(B, S, FOX_HEADS, FOX_DH).transpose(0, 2, 1, 3)
        log_f = jax.nn.log_sigmoid((fg + b_f[l]).astype(f32))
        cum = jnp.cumsum(log_f, axis=1).transpose(0, 2, 1)
        cum_blocks = cum.reshape(B, FOX_HEADS, nb, QBLOCK).transpose(2, 0, 1, 3)

        def block(args):
            i, q1b, q2b, qfb, cqb = args
            qpos = i * QBLOCK + jnp.arange(QBLOCK, dtype=jnp.int32)
            chunk_mask = k_chunk[None, :] <= (qpos // CHUNK)[:, None]
            causal = pos[None, :] <= qpos[:, None]
            s1 = jnp.einsum('bhqd,bhkd->bhqk', q1b, k1, preferred_element_type=f32) * scale_d
            s2 = jnp.einsum('bhqd,bhkd->bhqk', q2b, k2, preferred_element_type=f32) * scale_d
            p = (jax.nn.softmax(jnp.where(chunk_mask, s1, NEG), axis=-1)
                 - lam * jax.nn.softmax(jnp.where(chunk_mask, s2, NEG), axis=-1))
            od = jnp.einsum('bhqk,bhkd->bhqd', p.astype(dv_t.dtype), dv_t)
            sf = (jnp.einsum('bhqd,bhkd->bhqk', qfb, fk_t, preferred_element_type=f32) * scale_f
                  + cqb[:, :, :, None] - cum[:, :, None, :])
            pf = jax.nn.softmax(jnp.where(causal, sf, NEG), axis=-1)
            of = jnp.einsum('bhqk,bhkd->bhqd', pf.astype(fv_t.dtype), fv_t)
            return od, of

        od_b, of_b = lax.map(block, (jnp.arange(nb, dtype=jnp.int32),
                                     to_blocks(dq[:, :, :, 0], nb),
                                     to_blocks(dq[:, :, :, 1], nb),
                                     to_blocks(fq, nb),
                                     cum_blocks))
        od_b = rmsnorm(od_b, subln_g[l]) * (1.0 - lam_init)
        mix = jnp.concatenate([from_blocks(od_b), from_blocks(of_b)], axis=-1)
        x = x + jnp.einsum('bse,ed->bsd', mix, w_out[l])

        h2 = rmsnorm(x, norm2_g[l])
        g = jnp.einsum('bsd,df->bsf', h2, w_gate[l])
        u = jnp.einsum('bsd,df->bsf', h2, w_up[l])
        x = x + jnp.einsum('bsf,fd->bsd', jax.nn.silu(g) * u, w_down[l])

    return rmsnorm(x, normf_g)
```

```python
import functools
import math

import jax
import jax.numpy as jnp
import numpy as np
from jax import lax
from jax.experimental import pallas as pl
from jax.experimental.pallas import tpu as pltpu

D_MODEL = 1024
CHUNK = 64
DIFF_HEADS = 4
DIFF_DH = 64
DIFF_VD = 128
FOX_HEADS = 8
FOX_DH = 64
ROPE_THETA = 500000.0
ROPE_DIM = 16
ROPE_HALF = 8
D_FF = 2816
EPS = 1e-5
NEG = -1e30
LAM_INIT = 0.8 - 0.6 * math.exp(-0.3 * 0)
LOG2E = 1.4426950408889634

LANES = 128
MXU_N = 256
VMEM_LIMIT_BYTES = 56 * 1024 * 1024

PROJ_TM = 512
ATT_TQ = MXU_N
ATT_TK = 256
FFN_TM = 512

NAT_DK = DIFF_HEADS * 2 * DIFF_DH
NAT_FK = FOX_HEADS * LANES
NAT_W = NAT_DK + NAT_FK + LANES
TR_DQ = DIFF_HEADS * 2 * DIFF_DH
TR_DV = DIFF_HEADS * DIFF_VD
TR_FQ = FOX_HEADS * LANES
TR_FV = FOX_HEADS * FOX_DH
TR_FG = 16
TR_W = TR_DQ + TR_DV + TR_FQ + TR_FV + TR_FG
AUG_ONES = FOX_DH
AUG_BIAS = FOX_DH + 3


def _nt_dot(a, b):
    return lax.dot_general(a, b, (((1,), (1,)), ((), ())), preferred_element_type=jnp.float32)


def _split3(x):
    hi = x.astype(jnp.bfloat16)
    r = x - hi.astype(jnp.float32)
    mid = r.astype(jnp.bfloat16)
    lo = (r - mid.astype(jnp.float32)).astype(jnp.bfloat16)
    return hi, mid, lo


def _log_sigmoid(x):
    return jnp.minimum(x, 0.0) - jnp.log1p(jnp.exp(-jnp.abs(x)))


def _proj_kernel(x_ref, g_ref, wnat_ref, wtr_ref, bfn_ref, bft_ref, cn_ref, s1n_ref, s2n_ref,
                 ct_ref, st_ref, tri_ref, en_ref, pt_ref, onesn_ref,
                 kd_ref, kf_ref, qd_ref, vd_ref, qf_ref, vf_ref,
                 carry_n, carry_t):
    @pl.when(pl.program_id(1) == 0)
    def _():
        carry_n[...] = jnp.zeros_like(carry_n)
        carry_t[...] = jnp.zeros_like(carry_t)

    tm = x_ref.shape[1]
    x = x_ref[0]
    ms = jnp.mean(x * x, axis=-1, keepdims=True)
    h = (x * lax.rsqrt(ms + EPS) * g_ref[...]).astype(jnp.bfloat16)

    nat = jnp.dot(h, wnat_ref[...], preferred_element_type=jnp.float32)
    tr = _nt_dot(wtr_ref[...], h)

    cn, s1n, s2n = cn_ref[...], s1n_ref[...], s2n_ref[...]
    for hd in range(DIFF_HEADS):
        t = nat[:, hd * LANES:(hd + 1) * LANES]
        rot = t * cn + pltpu.roll(t, LANES - ROPE_HALF, 1) * s1n + pltpu.roll(t, ROPE_HALF, 1) * s2n
        kd_ref[0, hd] = rot.astype(jnp.bfloat16)

    tri = tri_ref[...]

    lf_n = _log_sigmoid(nat[:, NAT_DK + NAT_FK:] + bfn_ref[...])
    c3 = jnp.dot(tri, jnp.concatenate(_split3(lf_n), axis=1), preferred_element_type=jnp.float32)
    cum_n = (c3[:, 2 * LANES:] + c3[:, LANES:2 * LANES]) + c3[:, :LANES] + carry_n[0:1, :]
    carry_n[...] = jnp.broadcast_to(cum_n[tm - 1:tm, :], carry_n.shape)
    kb = jnp.concatenate(_split3(cum_n * (-LOG2E)), axis=1)
    aug_n = jnp.dot(kb, en_ref[...], preferred_element_type=jnp.float32)
    kf = nat[:, NAT_DK:NAT_DK + NAT_FK] + aug_n + onesn_ref[...]
    for hd in range(FOX_HEADS):
        kf_ref[0, hd] = kf[:, hd * LANES:(hd + 1) * LANES].astype(jnp.bfloat16)

    ct, st = ct_ref[...], st_ref[...]
    pieces = []
    for comp in range(DIFF_HEADS * 2):
        base = comp * DIFF_DH
        t1 = tr[base:base + ROPE_HALF]
        t2 = tr[base + ROPE_HALF:base + ROPE_DIM]
        pieces += [t1 * ct - t2 * st, t2 * ct + t1 * st, tr[base + ROPE_DIM:base + DIFF_DH]]
    qd = jnp.concatenate(pieces, axis=0) * (DIFF_DH ** -0.5 * LOG2E)
    qd_ref[0] = qd.astype(jnp.bfloat16)
    vd_ref[0] = tr[TR_DQ:TR_DQ + TR_DV].astype(jnp.bfloat16)

    o_fq = TR_DQ + TR_DV
    o_fv = o_fq + TR_FQ
    o_fg = o_fv + TR_FV
    lf_t = _log_sigmoid(tr[o_fg:o_fg + TR_FG] + bft_ref[...])
    c3t = _nt_dot(jnp.concatenate(_split3(lf_t), axis=0), tri)
    cum_t = (c3t[2 * TR_FG:] + c3t[TR_FG:2 * TR_FG]) + c3t[:TR_FG] + carry_t[:, 0:1]
    carry_t[...] = jnp.broadcast_to(cum_t[:, tm - 1:tm], carry_t.shape)
    qb = jnp.concatenate(_split3(cum_t * LOG2E) + (jnp.ones((TR_FG, tm), jnp.bfloat16),), axis=0)
    aug_t = jnp.dot(pt_ref[...], qb, preferred_element_type=jnp.float32)
    qf = tr[o_fq:o_fq + TR_FQ] * (FOX_DH ** -0.5 * LOG2E) + aug_t
    qf_ref[0] = qf.astype(jnp.bfloat16)
    vf_ref[0] = tr[o_fv:o_fv + TR_FV].astype(jnp.bfloat16)


def _proj_call(x, g1, wnat, wtr, bfn, bft, cn, s1n, s2n, ct, st, tri, en, pt, onesn):
    B, S, D = x.shape
    tm = PROJ_TM
    const = lambda shape: pl.BlockSpec(shape, lambda b, i: (0,) * len(shape))
    bf16 = jnp.bfloat16
    return pl.pallas_call(
        _proj_kernel,
        grid=(B, S // tm),
        in_specs=[
            pl.BlockSpec((1, tm, D), lambda b, i: (b, i, 0)),
            const((1, D)),
            const((D, NAT_W)),
            const((TR_W, D)),
            const((1, LANES)),
            const((TR_FG, tm)),
            pl.BlockSpec((tm, LANES), lambda b, i: (i, 0)),
            pl.BlockSpec((tm, LANES), lambda b, i: (i, 0)),
            pl.BlockSpec((tm, LANES), lambda b, i: (i, 0)),
            pl.BlockSpec((ROPE_HALF, tm), lambda b, i: (0, i)),
            pl.BlockSpec((ROPE_HALF, tm), lambda b, i: (0, i)),
            const((tm, tm)),
            const((3 * LANES, NAT_FK)),
            const((TR_FQ, 4 * TR_FG)),
            const((1, NAT_FK)),
        ],
        out_specs=[
            pl.BlockSpec((1, DIFF_HEADS, tm, LANES), lambda b, i: (b, 0, i, 0)),
            pl.BlockSpec((1, FOX_HEADS, tm, LANES), lambda b, i: (b, 0, i, 0)),
            pl.BlockSpec((1, TR_DQ, tm), lambda b, i: (b, 0, i)),
            pl.BlockSpec((1, TR_DV, tm), lambda b, i: (b, 0, i)),
            pl.BlockSpec((1, TR_FQ, tm), lambda b, i: (b, 0, i)),
            pl.BlockSpec((1, TR_FV, tm), lambda b, i: (b, 0, i)),
        ],
        out_shape=[
            jax.ShapeDtypeStruct((B, DIFF_HEADS, S, LANES), bf16),
            jax.ShapeDtypeStruct((B, FOX_HEADS, S, LANES), bf16),
            jax.ShapeDtypeStruct((B, TR_DQ, S), bf16),
            jax.ShapeDtypeStruct((B, TR_DV, S), bf16),
            jax.ShapeDtypeStruct((B, TR_FQ, S), bf16),
            jax.ShapeDtypeStruct((B, TR_FV, S), bf16),
        ],
        scratch_shapes=[pltpu.VMEM((8, LANES), jnp.float32), pltpu.VMEM((TR_FG, LANES), jnp.float32)],
        compiler_params=pltpu.CompilerParams(
            dimension_semantics=("arbitrary", "arbitrary"), vmem_limit_bytes=VMEM_LIMIT_BYTES),
        name="proj",
    )(x, g1, wnat, wtr, bfn, bft, cn, s1n, s2n, ct, st, tri, en, pt, onesn)


def _flash_step(s, v_t, m_ref, l_ref, acc_ref):
    m_old = m_ref[...]
    m_new = jnp.maximum(m_old, jnp.max(s, axis=0, keepdims=True))
    alpha = jnp.exp2(m_old - m_new)
    p = jnp.exp2(s - m_new)
    l_ref[...] = alpha * l_ref[...] + jnp.sum(p, axis=0, keepdims=True)
    acc_ref[...] = alpha * acc_ref[...] + jnp.dot(
        v_t, p.astype(jnp.bfloat16), preferred_element_type=jnp.float32)
    m_ref[...] = m_new


def _init_stats(m_ref, l_ref, acc_ref):
    m_ref[...] = jnp.full_like(m_ref, NEG)
    l_ref[...] = jnp.zeros_like(l_ref)
    acc_ref[...] = jnp.zeros_like(acc_ref)


def _dattn_kernel(q_ref, k_ref, v_ref, lq1_ref, lk1_ref, lq2_ref, lk2_ref, g_ref, o_ref,
                  m1, l1, a1, m2, l2, a2):
    i = pl.program_id(2)
    tq, tk = ATT_TQ, ATT_TK
    q = q_ref[0]
    row = lax.broadcasted_iota(jnp.int32, q.shape, 0)
    zero = jnp.zeros_like(q)
    rhs1 = jnp.where(row < DIFF_DH, q, zero)
    rhs2 = jnp.where(row >= DIFF_DH, q, zero)

    _init_stats(m1, l1, a1)
    _init_stats(m2, l2, a2)

    def step(j, masked):
        k = k_ref[0, 0, pl.ds(pl.multiple_of(j * tk, tk), tk), :]
        v_t = v_ref[0, :, pl.ds(pl.multiple_of(j * tk, tk), tk)]
        s1 = jnp.dot(k, rhs1, preferred_element_type=jnp.float32)
        s2 = jnp.dot(k, rhs2, preferred_element_type=jnp.float32)
        if masked:
            kk = lax.broadcasted_iota(jnp.int32, (tk, tq), 0) // CHUNK
            qq = lax.broadcasted_iota(jnp.int32, (tk, tq), 1) // CHUNK
            ok = kk <= qq
            s1 = jnp.where(ok, s1, NEG)
            s2 = jnp.where(ok, s2, NEG)
        _flash_step(s1, v_t, m1, l1, a1)
        _flash_step(s2, v_t, m2, l2, a2)

    def body(j, c):
        step(j, False)
        return c

    lax.fori_loop(0, i, body, 0)
    step(i, True)

    lam = (jnp.exp(jnp.sum(lq1_ref[...] * lk1_ref[...], axis=-1, keepdims=True))
           - jnp.exp(jnp.sum(lq2_ref[...] * lk2_ref[...], axis=-1, keepdims=True)) + LAM_INIT)
    o = a1[...] / l1[...] - lam * (a2[...] / l2[...])
    ms = jnp.mean(o * o, axis=0, keepdims=True)
    o = o * lax.rsqrt(ms + EPS) * g_ref[...] * (1.0 - LAM_INIT)
    o_ref[0] = o.T.astype(o_ref.dtype)


def _dattn_call(qd, kd, vd, lq1, lk1, lq2, lk2, gcol):
    B, _, S = qd.shape
    tq = ATT_TQ
    vec = lambda: pl.BlockSpec((1, DIFF_DH), lambda b, h, i: (0, 0))
    stat = lambda: pltpu.VMEM((1, tq), jnp.float32)
    acc = lambda: pltpu.VMEM((DIFF_VD, tq), jnp.float32)
    return pl.pallas_call(
        _dattn_kernel,
        grid=(B, DIFF_HEADS, S // tq),
        in_specs=[
            pl.BlockSpec((1, 2 * DIFF_DH, tq), lambda b, h, i: (b, h, i)),
            pl.BlockSpec((1, 1, S, LANES), lambda b, h, i: (b, h, 0, 0)),
            pl.BlockSpec((1, DIFF_VD, S), lambda b, h, i: (b, h, 0)),
            vec(), vec(), vec(), vec(),
            pl.BlockSpec((DIFF_VD, tq), lambda b, h, i: (0, 0)),
        ],
        out_specs=pl.BlockSpec((1, tq, DIFF_VD), lambda b, h, i: (b, i, h)),
        out_shape=jax.ShapeDtypeStruct((B, S, DIFF_HEADS * DIFF_VD), jnp.bfloat16),
        scratch_shapes=[stat(), stat(), acc(), stat(), stat(), acc()],
        compiler_params=pltpu.CompilerParams(
            dimension_semantics=("arbitrary", "arbitrary", "arbitrary"),
            vmem_limit_bytes=VMEM_LIMIT_BYTES),
        name="dattn",
    )(qd, kd, vd, lq1, lk1, lq2, lk2, gcol)


def _fattn_kernel(q_ref, k_ref, v_ref, o_ref, ma, la, aa, mb, lb, ab):
    i = pl.program_id(2)
    tq, tk = ATT_TQ, ATT_TK
    stats = ((ma, la, aa), (mb, lb, ab))
    for st in stats:
        _init_stats(*st)

    def step(j, masked):
        ks = pl.ds(pl.multiple_of(j * tk, tk), tk)
        for hh, st in enumerate(stats):
            k = k_ref[0, hh, ks, :]
            q = q_ref[0, hh * LANES:(hh + 1) * LANES, :]
            v_t = v_ref[0, hh * FOX_DH:(hh + 1) * FOX_DH, ks]
            s = jnp.dot(k, q, preferred_element_type=jnp.float32)
            if masked:
                kk = lax.broadcasted_iota(jnp.int32, (tk, tq), 0)
                qq = lax.broadcasted_iota(jnp.int32, (tk, tq), 1)
                s = jnp.where(kk <= qq, s, NEG)
            _flash_step(s, v_t, *st)

    def body(j, c):
        step(j, False)
        return c

    lax.fori_loop(0, i, body, 0)
    step(i, True)

    o = jnp.concatenate([aa[...] / la[...], ab[...] / lb[...]], axis=0)
    o_ref[0] = o.T.astype(o_ref.dtype)


def _fattn_call(qf, kf, vf):
    B, _, S = qf.shape
    tq = ATT_TQ
    stat = lambda: pltpu.VMEM((1, tq), jnp.float32)
    acc = lambda: pltpu.VMEM((FOX_DH, tq), jnp.float32)
    return pl.pallas_call(
        _fattn_kernel,
        grid=(B, FOX_HEADS // 2, S // tq),
        in_specs=[
            pl.BlockSpec((1, 2 * LANES, tq), lambda b, p, i: (b, p, i)),
            pl.BlockSpec((1, 2, S, LANES), lambda b, p, i: (b, p, 0, 0)),
            pl.BlockSpec((1, 2 * FOX_DH, S), lambda b, p, i: (b, p, 0)),
        ],
        out_specs=pl.BlockSpec((1, tq, 2 * FOX_DH), lambda b, p, i: (b, i, p)),
        out_shape=jax.ShapeDtypeStruct((B, S, FOX_HEADS * FOX_DH), jnp.bfloat16),
        scratch_shapes=[stat(), stat(), acc(), stat(), stat(), acc()],
        compiler_params=pltpu.CompilerParams(
            dimension_semantics=("arbitrary", "arbitrary", "arbitrary"),
            vmem_limit_bytes=VMEM_LIMIT_BYTES),
        name="fattn",
    )(qf, kf, vf)


def _ffn_kernel(x_ref, md_ref, mf_ref, wod_ref, wof_ref, g2_ref, wg_ref, wu_ref, wd_ref, gf_ref, o_ref):
    x1 = (x_ref[...]
          + jnp.dot(md_ref[...], wod_ref[...], preferred_element_type=jnp.float32)
          + jnp.dot(mf_ref[...], wof_ref[...], preferred_element_type=jnp.float32))
    ms = jnp.mean(x1 * x1, axis=-1, keepdims=True)
    h2 = (x1 * lax.rsqrt(ms + EPS) * g2_ref[...]).astype(jnp.bfloat16)
    g = jnp.dot(h2, wg_ref[...], preferred_element_type=jnp.float32)
    u = jnp.dot(h2, wu_ref[...], preferred_element_type=jnp.float32)
    act = (g * jax.nn.sigmoid(g) * u).astype(jnp.bfloat16)
    y = x1 + jnp.dot(act, wd_ref[...], preferred_element_type=jnp.float32)
    ms = jnp.mean(y * y, axis=-1, keepdims=True)
    o_ref[...] = y * lax.rsqrt(ms + EPS) * gf_ref[...]


def _ffn_call(x2, md, mf, wod, wof, g2, wg, wu, wd, gf):
    n, D = x2.shape
    tm = FFN_TM
    half = md.shape[1]
    resident = lambda shape: pl.BlockSpec(shape, lambda i: (0, 0), pipeline_mode=pl.Buffered(1))
    return pl.pallas_call(
        _ffn_kernel,
        grid=(n // tm,),
        in_specs=[
            pl.BlockSpec((tm, D), lambda i: (i, 0)),
            pl.BlockSpec((tm, half), lambda i: (i, 0)),
            pl.BlockSpec((tm, half), lambda i: (i, 0)),
            resident((half, D)),
            resident((half, D)),
            resident((1, D)),
            resident((D, D_FF)),
            resident((D, D_FF)),
            resident((D_FF, D)),
            resident((1, D)),
        ],
        out_specs=pl.BlockSpec((tm, D), lambda i: (i, 0)),
        out_shape=jax.ShapeDtypeStruct((n, D), jnp.float32),
        compiler_params=pltpu.CompilerParams(
            dimension_semantics=("arbitrary",), vmem_limit_bytes=VMEM_LIMIT_BYTES),
        name="ffn",
    )(x2, md, mf, wod, wof, g2, wg, wu, wd, gf)


def _placement_tables():
    en = np.zeros((3 * LANES, NAT_FK), np.float32)
    pt = np.zeros((TR_FQ, 4 * TR_FG), np.float32)
    onesn = np.zeros((1, NAT_FK), np.float32)
    for h in range(FOX_HEADS):
        for p in range(3):
            en[p * LANES + h, h * LANES + AUG_BIAS + p] = 1.0
            onesn[0, h * LANES + AUG_ONES + p] = 1.0
            pt[h * LANES + AUG_ONES + p, p * TR_FG + h] = 1.0
            pt[h * LANES + AUG_BIAS + p, 3 * TR_FG] = 1.0
    tri = np.tril(np.ones((PROJ_TM, PROJ_TM), np.float32))
    return en, pt, onesn, tri


def _rope_tables(S):
    f32 = jnp.float32
    pos = jnp.arange(S, dtype=jnp.int32)
    inv_freq = ROPE_THETA ** (-jnp.arange(0, ROPE_DIM, 2, dtype=f32) / ROPE_DIM)
    ang = pos.astype(f32)[:, None] * inv_freq[None, :]
    cos, sin = jnp.cos(ang), jnp.sin(ang)
    lane = np.arange(LANES) % DIFF_DH
    sel = lane % ROPE_HALF
    lo = jnp.asarray(lane < ROPE_HALF)[None, :]
    hi = jnp.asarray((lane >= ROPE_HALF) & (lane < ROPE_DIM))[None, :]
    cos_l, sin_l = cos[:, sel], sin[:, sel]
    cn = jnp.where(lo | hi, cos_l, 1.0)
    s1n = jnp.where(lo, -sin_l, 0.0)
    s2n = jnp.where(hi, sin_l, 0.0)
    return cn, s1n, s2n, cos.T, sin.T


def kernel(x, norm1_g, w_in, b_f, lam_q1, lam_k1, lam_q2, lam_k2, subln_g, w_out, norm2_g, w_gate, w_up,
           w_down, normf_g):
    f32, bf16 = jnp.float32, jnp.bfloat16
    B, S, D = x.shape
    en, pt, onesn, tri = _placement_tables()
    cn, s1n, s2n, ct, st = _rope_tables(S)

    w = w_in[0]
    o_dq, o_dk, o_dv, o_fq, o_fk, o_fv, o_fg = 0, 512, 1024, 1536, 2048, 2560, 3072
    pad_heads = lambda m: jnp.pad(m.reshape(D, FOX_HEADS, FOX_DH), ((0, 0), (0, 0), (0, LANES - FOX_DH))
                                  ).reshape(D, FOX_HEADS * LANES)
    wnat = jnp.concatenate(
        [w[:, o_dk:o_dv], pad_heads(w[:, o_fk:o_fv]), jnp.pad(w[:, o_fg:], ((0, 0), (0, LANES - FOX_HEADS)))],
        axis=1).astype(bf16)
    wtr = jnp.concatenate(
        [w[:, o_dq:o_dk], w[:, o_dv:o_fq], pad_heads(w[:, o_fq:o_fk]), w[:, o_fv:o_fg],
         jnp.pad(w[:, o_fg:], ((0, 0), (0, TR_FG - FOX_HEADS)))], axis=1).T.astype(bf16)
    bfn = jnp.pad(b_f[0].astype(f32), (0, LANES - FOX_HEADS))[None, :]
    bft = jnp.broadcast_to(jnp.pad(b_f[0].astype(f32), (0, TR_FG - FOX_HEADS))[:, None], (TR_FG, PROJ_TM))

    kd, kf, qd, vd, qf, vf = _proj_call(
        x, norm1_g[0][None, :], wnat, wtr, bfn, bft, cn, s1n, s2n, ct, st,
        jnp.asarray(tri, bf16), jnp.asarray(en, bf16), jnp.asarray(pt, bf16), jnp.asarray(onesn, f32))

    gcol = jnp.broadcast_to(subln_g[0].astype(f32)[:, None], (DIFF_VD, ATT_TQ))
    row = lambda v: v[0].astype(f32)[None, :]
    mixd = _dattn_call(qd, kd, vd, row(lam_q1), row(lam_k1), row(lam_q2), row(lam_k2), gcol)
    mixf = _fattn_call(qf, kf, vf)

    wo = w_out[0].astype(bf16)
    half = DIFF_HEADS * DIFF_VD
    out = _ffn_call(
        x.reshape(B * S, D), mixd.reshape(B * S, half), mixf.reshape(B * S, half),
        wo[:half], wo[half:], norm2_g[0][None, :], w_gate[0].astype(bf16), w_up[0].astype(bf16),
        w_down[0].astype(bf16), normf_g[None, :])
    return out.reshape(B, S, D)
```

```python
import math

import jax
import jax.numpy as jnp
import numpy as np
from jax import lax
from jax.experimental import pallas as pl
from jax.experimental.pallas import tpu as pltpu

D_MODEL = 1024
CHUNK = 64
DIFF_HEADS = 4
DIFF_DH = 64
DIFF_VD = 128
FOX_HEADS = 8
FOX_DH = 64
ROPE_THETA = 500000.0
ROPE_DIM = 16
ROPE_HALF = 8
D_FF = 2816
EPS = 1e-5
NEG = -1e30
LAM_INIT = 0.8 - 0.6 * math.exp(-0.3 * 0)
LOG2E = 1.4426950408889634

LANES = 128
MXU_N = 256
VMEM_LIMIT_BYTES = 56 * 1024 * 1024

PROJ_TM = 512
ATT_TQ = MXU_N
ATT_TK = 512
FFN_TM = 512

NAT_DK = DIFF_HEADS * 2 * DIFF_DH
NAT_FK = FOX_HEADS * LANES
NAT_W = NAT_DK + NAT_FK + LANES
TR_DQ = DIFF_HEADS * 2 * DIFF_DH
TR_DV = DIFF_HEADS * DIFF_VD
TR_FQ = FOX_HEADS * LANES
TR_FV = FOX_HEADS * FOX_DH
TR_FG = 16
TR_W = TR_DQ + TR_DV + TR_FQ + TR_FV + TR_FG
AUG_ONES = FOX_DH
AUG_BIAS = FOX_DH + 3


def _nt_dot(a, b):
    return lax.dot_general(a, b, (((1,), (1,)), ((), ())), preferred_element_type=jnp.float32)


def _split3(x):
    hi = x.astype(jnp.bfloat16)
    r = x - hi.astype(jnp.float32)
    mid = r.astype(jnp.bfloat16)
    lo = (r - mid.astype(jnp.float32)).astype(jnp.bfloat16)
    return hi, mid, lo


def _log_sigmoid(x):
    return jnp.minimum(x, 0.0) - jnp.log1p(jnp.exp(-jnp.abs(x)))


def _proj_kernel(x_ref, g_ref, wnat_ref, wtr_ref, bfn_ref, bft_ref, cn_ref, s1n_ref, s2n_ref,
                 ct_ref, st_ref, tri_ref, en_ref, pt_ref, onesn_ref,
                 kd_ref, kf_ref, qd_ref, vd_ref, qf_ref, vf_ref,
                 carry_n, carry_t):
    @pl.when(pl.program_id(1) == 0)
    def _():
        carry_n[...] = jnp.zeros_like(carry_n)
        carry_t[...] = jnp.zeros_like(carry_t)

    tm = x_ref.shape[1]
    x = x_ref[0]
    ms = jnp.mean(x * x, axis=-1, keepdims=True)
    h = (x * lax.rsqrt(ms + EPS) * g_ref[...]).astype(jnp.bfloat16)

    nat = jnp.dot(h, wnat_ref[...], preferred_element_type=jnp.float32)
    tr = _nt_dot(wtr_ref[...], h)

    cn, s1n, s2n = cn_ref[...], s1n_ref[...], s2n_ref[...]
    for hd in range(DIFF_HEADS):
        t = nat[:, hd * LANES:(hd + 1) * LANES]
        rot = t * cn + pltpu.roll(t, LANES - ROPE_HALF, 1) * s1n + pltpu.roll(t, ROPE_HALF, 1) * s2n
        kd_ref[0, hd] = rot.astype(jnp.bfloat16)

    tri = tri_ref[...]

    lf_n = _log_sigmoid(nat[:, NAT_DK + NAT_FK:] + bfn_ref[...])
    c3 = jnp.dot(tri, jnp.concatenate(_split3(lf_n), axis=1), preferred_element_type=jnp.float32)
    cum_n = (c3[:, 2 * LANES:] + c3[:, LANES:2 * LANES]) + c3[:, :LANES] + carry_n[0:1, :]
    carry_n[...] = jnp.broadcast_to(cum_n[tm - 1:tm, :], carry_n.shape)
    kb = jnp.concatenate(_split3(cum_n * (-LOG2E)), axis=1)
    aug_n = jnp.dot(kb, en_ref[...], preferred_element_type=jnp.float32)
    kf = nat[:, NAT_DK:NAT_DK + NAT_FK] + aug_n + onesn_ref[...]
    for hd in range(FOX_HEADS):
        kf_ref[0, hd] = kf[:, hd * LANES:(hd + 1) * LANES].astype(jnp.bfloat16)

    ct, st = ct_ref[...], st_ref[...]
    pieces = []
    for comp in range(DIFF_HEADS * 2):
        base = comp * DIFF_DH
        t1 = tr[base:base + ROPE_HALF]
        t2 = tr[base + ROPE_HALF:base + ROPE_DIM]
        pieces += [t1 * ct - t2 * st, t2 * ct + t1 * st, tr[base + ROPE_DIM:base + DIFF_DH]]
    qd = jnp.concatenate(pieces, axis=0) * (DIFF_DH ** -0.5 * LOG2E)
    qd_ref[0] = qd.astype(jnp.bfloat16)
    vd_ref[0] = tr[TR_DQ:TR_DQ + TR_DV].astype(jnp.bfloat16)

    o_fq = TR_DQ + TR_DV
    o_fv = o_fq + TR_FQ
    o_fg = o_fv + TR_FV
    lf_t = _log_sigmoid(tr[o_fg:o_fg + TR_FG] + bft_ref[...])
    c3t = _nt_dot(jnp.concatenate(_split3(lf_t), axis=0), tri)
    cum_t = (c3t[2 * TR_FG:] + c3t[TR_FG:2 * TR_FG]) + c3t[:TR_FG] + carry_t[:, 0:1]
    carry_t[...] = jnp.broadcast_to(cum_t[:, tm - 1:tm], carry_t.shape)
    qb = jnp.concatenate(_split3(cum_t * LOG2E) + (jnp.ones((TR_FG, tm), jnp.bfloat16),), axis=0)
    aug_t = jnp.dot(pt_ref[...], qb, preferred_element_type=jnp.float32)
    qf = tr[o_fq:o_fq + TR_FQ] * (FOX_DH ** -0.5 * LOG2E) + aug_t
    qf_ref[0] = qf.astype(jnp.bfloat16)
    vf_ref[0] = tr[o_fv:o_fv + TR_FV].astype(jnp.bfloat16)


def _proj_call(x, g1, wnat, wtr, bfn, bft, cn, s1n, s2n, ct, st, tri, en, pt, onesn):
    B, S, D = x.shape
    tm = PROJ_TM
    const = lambda shape: pl.BlockSpec(shape, lambda b, i: (0,) * len(shape))
    bf16 = jnp.bfloat16
    return pl.pallas_call(
        _proj_kernel,
        grid=(B, S // tm),
        in_specs=[
            pl.BlockSpec((1, tm, D), lambda b, i: (b, i, 0)),
            const((1, D)),
            const((D, NAT_W)),
            const((TR_W, D)),
            const((1, LANES)),
            const((TR_FG, tm)),
            pl.BlockSpec((tm, LANES), lambda b, i: (i, 0)),
            pl.BlockSpec((tm, LANES), lambda b, i: (i, 0)),
            pl.BlockSpec((tm, LANES), lambda b, i: (i, 0)),
            pl.BlockSpec((ROPE_HALF, tm), lambda b, i: (0, i)),
            pl.BlockSpec((ROPE_HALF, tm), lambda b, i: (0, i)),
            const((tm, tm)),
            const((3 * LANES, NAT_FK)),
            const((TR_FQ, 4 * TR_FG)),
            const((1, NAT_FK)),
        ],
        out_specs=[
            pl.BlockSpec((1, DIFF_HEADS, tm, LANES), lambda b, i: (b, 0, i, 0)),
            pl.BlockSpec((1, FOX_HEADS, tm, LANES), lambda b, i: (b, 0, i, 0)),
            pl.BlockSpec((1, TR_DQ, tm), lambda b, i: (b, 0, i)),
            pl.BlockSpec((1, TR_DV, tm), lambda b, i: (b, 0, i)),
            pl.BlockSpec((1, TR_FQ, tm), lambda b, i: (b, 0, i)),
            pl.BlockSpec((1, TR_FV, tm), lambda b, i: (b, 0, i)),
        ],
        out_shape=[
            jax.ShapeDtypeStruct((B, DIFF_HEADS, S, LANES), bf16),
            jax.ShapeDtypeStruct((B, FOX_HEADS, S, LANES), bf16),
            jax.ShapeDtypeStruct((B, TR_DQ, S), bf16),
            jax.ShapeDtypeStruct((B, TR_DV, S), bf16),
            jax.ShapeDtypeStruct((B, TR_FQ, S), bf16),
            jax.ShapeDtypeStruct((B, TR_FV, S), bf16),
        ],
        scratch_shapes=[pltpu.VMEM((8, LANES), jnp.float32), pltpu.VMEM((TR_FG, LANES), jnp.float32)],
        compiler_params=pltpu.CompilerParams(
            dimension_semantics=("arbitrary", "arbitrary"), vmem_limit_bytes=VMEM_LIMIT_BYTES),
        name="proj",
    )(x, g1, wnat, wtr, bfn, bft, cn, s1n, s2n, ct, st, tri, en, pt, onesn)


def _sweep_tables(S, chunk):
    nq, ratio = S // ATT_TQ, ATT_TK // ATT_TQ
    its, jts, mids = [], [], []
    for i in range(nq):
        nb = i // ratio + 1
        for j in range(nb):
            its.append(i)
            jts.append(j)
            mids.append(0 if j < nb - 1 else 1 + i % ratio)
    tab = np.array([its, jts, mids], np.int32)
    kk = np.arange(ATT_TK)[:, None] // chunk
    masks = [np.zeros((ATT_TK, ATT_TQ), np.float32)]
    for r in range(ratio):
        qq = (r * ATT_TQ + np.arange(ATT_TQ))[None, :] // chunk
        masks.append(np.where(kk <= qq, 0.0, NEG).astype(np.float32))
    return tab, np.stack(masks)


def _flat_sweep(tab_ref, n_pairs, n_heads, scores, values, mask_ref, m_s, l_s, acc_s, sbuf, pbuf, abuf):
    m_s[...] = jnp.full_like(m_s, NEG)
    l_s[...] = jnp.zeros_like(l_s)
    acc_s[...] = jnp.zeros_like(acc_s)

    def stage_scores(t):
        it, jt, mid = tab_ref[0, t], tab_ref[1, t], tab_ref[2, t]
        bias = mask_ref[mid]
        for h in range(n_heads):
            sbuf[t % 2, h] = scores(h, it, jt) + bias

    def stage_softmax(t):
        it = tab_ref[0, t]
        for h in range(n_heads):
            s = sbuf[t % 2, h]
            m_old = m_s[h, it]
            m_new = jnp.maximum(m_old, jnp.max(s, axis=0, keepdims=True))
            alpha = jnp.exp2(m_old - m_new)
            p = jnp.exp2(s - m_new)
            l_s[h, it] = alpha * l_s[h, it] + jnp.sum(p, axis=0, keepdims=True)
            m_s[h, it] = m_new
            pbuf[t % 2, h] = p.astype(jnp.bfloat16)
            abuf[t % 2, h] = alpha

    def stage_values(t):
        it, jt = tab_ref[0, t], tab_ref[1, t]
        for h in range(n_heads):
            pv = jnp.dot(values(h, jt), pbuf[t % 2, h], preferred_element_type=jnp.float32)
            acc_s[h, it] = abuf[t % 2, h] * acc_s[h, it] + pv

    stage_scores(0)
    stage_softmax(0)
    stage_scores(1)

    def body(t, c):
        stage_values(t - 2)
        stage_softmax(t - 1)
        stage_scores(t)
        return c

    lax.fori_loop(2, n_pairs, body, 0)
    stage_values(n_pairs - 2)
    stage_softmax(n_pairs - 1)
    stage_values(n_pairs - 1)


def _sweep_scratch(n_heads, nq, dv):
    f32 = jnp.float32
    return [
        pltpu.VMEM((n_heads, nq, 1, ATT_TQ), f32),
        pltpu.VMEM((n_heads, nq, 1, ATT_TQ), f32),
        pltpu.VMEM((n_heads, nq, dv, ATT_TQ), f32),
        pltpu.VMEM((2, n_heads, ATT_TK, ATT_TQ), f32),
        pltpu.VMEM((2, n_heads, ATT_TK, ATT_TQ), jnp.bfloat16),
        pltpu.VMEM((2, n_heads, 1, ATT_TQ), f32),
    ]


def _key_slice(jt):
    return pl.ds(pl.multiple_of(jt * ATT_TK, ATT_TK), ATT_TK)


def _query_slice(it):
    return pl.ds(pl.multiple_of(it * ATT_TQ, ATT_TQ), ATT_TQ)


def _dattn_kernel(tab_ref, q_ref, k_ref, v_ref, mask_ref, lq1_ref, lk1_ref, lq2_ref, lk2_ref, g_ref, o_ref,
                  m_s, l_s, acc_s, sbuf, pbuf, abuf):
    n_pairs = tab_ref.shape[1]
    nq = m_s.shape[1]
    row = lax.broadcasted_iota(jnp.int32, (2 * DIFF_DH, ATT_TQ), 0)

    def scores(h, it, jt):
        q = q_ref[0, :, _query_slice(it)]
        keep = (row < DIFF_DH) if h == 0 else (row >= DIFF_DH)
        rhs = jnp.where(keep, q, jnp.zeros_like(q))
        return jnp.dot(k_ref[0, 0, _key_slice(jt), :], rhs, preferred_element_type=jnp.float32)

    def values(h, jt):
        return v_ref[0, :, _key_slice(jt)]

    _flat_sweep(tab_ref, n_pairs, 2, scores, values, mask_ref, m_s, l_s, acc_s, sbuf, pbuf, abuf)

    lam = (jnp.exp(jnp.sum(lq1_ref[...] * lk1_ref[...], axis=-1, keepdims=True))
           - jnp.exp(jnp.sum(lq2_ref[...] * lk2_ref[...], axis=-1, keepdims=True)) + LAM_INIT)
    gain = g_ref[...] * (1.0 - LAM_INIT)

    def finalize(it, c):
        o = acc_s[0, it] / l_s[0, it] - lam * (acc_s[1, it] / l_s[1, it])
        ms = jnp.mean(o * o, axis=0, keepdims=True)
        o = o * lax.rsqrt(ms + EPS) * gain
        o_ref[0, _query_slice(it), :] = o.T.astype(o_ref.dtype)
        return c

    lax.fori_loop(0, nq, finalize, 0)


def _dattn_call(tab, masks, qd, kd, vd, lq1, lk1, lq2, lk2, gcol):
    B, _, S = qd.shape
    nq = S // ATT_TQ
    vec = lambda: pl.BlockSpec((1, DIFF_DH), lambda b, h, tab: (0, 0))
    return pl.pallas_call(
        _dattn_kernel,
        grid_spec=pltpu.PrefetchScalarGridSpec(
            num_scalar_prefetch=1,
            grid=(B, DIFF_HEADS),
            in_specs=[
                pl.BlockSpec((1, 2 * DIFF_DH, S), lambda b, h, tab: (b, h, 0)),
                pl.BlockSpec((1, 1, S, LANES), lambda b, h, tab: (b, h, 0, 0)),
                pl.BlockSpec((1, DIFF_VD, S), lambda b, h, tab: (b, h, 0)),
                pl.BlockSpec(masks.shape, lambda b, h, tab: (0, 0, 0)),
                vec(), vec(), vec(), vec(),
                pl.BlockSpec((DIFF_VD, ATT_TQ), lambda b, h, tab: (0, 0)),
            ],
            out_specs=pl.BlockSpec((1, S, DIFF_VD), lambda b, h, tab: (b, 0, h)),
            scratch_shapes=_sweep_scratch(2, nq, DIFF_VD),
        ),
        out_shape=jax.ShapeDtypeStruct((B, S, DIFF_HEADS * DIFF_VD), jnp.bfloat16),
        compiler_params=pltpu.CompilerParams(
            dimension_semantics=("arbitrary", "arbitrary"), vmem_limit_bytes=VMEM_LIMIT_BYTES),
        name="dattn",
    )(tab, qd, kd, vd, masks, lq1, lk1, lq2, lk2, gcol)


def _fattn_kernel(tab_ref, q_ref, k_ref, v_ref, mask_ref, o_ref, m_s, l_s, acc_s, sbuf, pbuf, abuf):
    n_pairs = tab_ref.shape[1]
    nq = m_s.shape[1]

    def scores(h, it, jt):
        q = q_ref[0, h * LANES:(h + 1) * LANES, _query_slice(it)]
        return jnp.dot(k_ref[0, h, _key_slice(jt), :], q, preferred_element_type=jnp.float32)

    def values(h, jt):
        return v_ref[0, h * FOX_DH:(h + 1) * FOX_DH, _key_slice(jt)]

    _flat_sweep(tab_ref, n_pairs, 2, scores, values, mask_ref, m_s, l_s, acc_s, sbuf, pbuf, abuf)

    def finalize(it, c):
        o = jnp.concatenate([acc_s[0, it] / l_s[0, it], acc_s[1, it] / l_s[1, it]], axis=0)
        o_ref[0, _query_slice(it), :] = o.T.astype(o_ref.dtype)
        return c

    lax.fori_loop(0, nq, finalize, 0)


def _fattn_call(tab, masks, qf, kf, vf):
    B, _, S = qf.shape
    nq = S // ATT_TQ
    return pl.pallas_call(
        _fattn_kernel,
        grid_spec=pltpu.PrefetchScalarGridSpec(
            num_scalar_prefetch=1,
            grid=(B, FOX_HEADS // 2),
            in_specs=[
                pl.BlockSpec((1, 2 * LANES, S), lambda b, p, tab: (b, p, 0)),
                pl.BlockSpec((1, 2, S, LANES), lambda b, p, tab: (b, p, 0, 0)),
                pl.BlockSpec((1, 2 * FOX_DH, S), lambda b, p, tab: (b, p, 0)),
                pl.BlockSpec(masks.shape, lambda b, p, tab: (0, 0, 0)),
            ],
            out_specs=pl.BlockSpec((1, S, 2 * FOX_DH), lambda b, p, tab: (b, 0, p)),
            scratch_shapes=_sweep_scratch(2, nq, FOX_DH),
        ),
        out_shape=jax.ShapeDtypeStruct((B, S, FOX_HEADS * FOX_DH), jnp.bfloat16),
        compiler_params=pltpu.CompilerParams(
            dimension_semantics=("arbitrary", "arbitrary"), vmem_limit_bytes=VMEM_LIMIT_BYTES),
        name="fattn",
    )(tab, qf, kf, vf, masks)


def _ffn_kernel(x_ref, md_ref, mf_ref, wod_ref, wof_ref, g2_ref, wg_ref, wu_ref, wd_ref, gf_ref, o_ref):
    x1 = (x_ref[...]
          + jnp.dot(md_ref[...], wod_ref[...], preferred_element_type=jnp.float32)
          + jnp.dot(mf_ref[...], wof_ref[...], preferred_element_type=jnp.float32))
    ms = jnp.mean(x1 * x1, axis=-1, keepdims=True)
    h2 = (x1 * lax.rsqrt(ms + EPS) * g2_ref[...]).astype(jnp.bfloat16)
    g = jnp.dot(h2, wg_ref[...], preferred_element_type=jnp.float32)
    u = jnp.dot(h2, wu_ref[...], preferred_element_type=jnp.float32)
    act = (g * jax.nn.sigmoid(g) * u).astype(jnp.bfloat16)
    y = x1 + jnp.dot(act, wd_ref[...], preferred_element_type=jnp.float32)
    ms = jnp.mean(y * y, axis=-1, keepdims=True)
    o_ref[...] = y * lax.rsqrt(ms + EPS) * gf_ref[...]


def _ffn_call(x2, md, mf, wod, wof, g2, wg, wu, wd, gf):
    n, D = x2.shape
    tm = FFN_TM
    half = md.shape[1]
    resident = lambda shape: pl.BlockSpec(shape, lambda i: (0, 0), pipeline_mode=pl.Buffered(1))
    return pl.pallas_call(
        _ffn_kernel,
        grid=(n // tm,),
        in_specs=[
            pl.BlockSpec((tm, D), lambda i: (i, 0)),
            pl.BlockSpec((tm, half), lambda i: (i, 0)),
            pl.BlockSpec((tm, half), lambda i: (i, 0)),
            resident((half, D)),
            resident((half, D)),
            resident((1, D)),
            resident((D, D_FF)),
            resident((D, D_FF)),
            resident((D_FF, D)),
            resident((1, D)),
        ],
        out_specs=pl.BlockSpec((tm, D), lambda i: (i, 0)),
        out_shape=jax.ShapeDtypeStruct((n, D), jnp.float32),
        compiler_params=pltpu.CompilerParams(
            dimension_semantics=("arbitrary",), vmem_limit_bytes=VMEM_LIMIT_BYTES),
        name="ffn",
    )(x2, md, mf, wod, wof, g2, wg, wu, wd, gf)


def _placement_tables():
    en = np.zeros((3 * LANES, NAT_FK), np.float32)
    pt = np.zeros((TR_FQ, 4 * TR_FG), np.float32)
    onesn = np.zeros((1, NAT_FK), np.float32)
    for h in range(FOX_HEADS):
        for p in range(3):
            en[p * LANES + h, h * LANES + AUG_BIAS + p] = 1.0
            onesn[0, h * LANES + AUG_ONES + p] = 1.0
            pt[h * LANES + AUG_ONES + p, p * TR_FG + h] = 1.0
            pt[h * LANES + AUG_BIAS + p, 3 * TR_FG] = 1.0
    tri = np.tril(np.ones((PROJ_TM, PROJ_TM), np.float32))
    return en, pt, onesn, tri


def _rope_tables(S):
    f32 = jnp.float32
    pos = jnp.arange(S, dtype=jnp.int32)
    inv_freq = ROPE_THETA ** (-jnp.arange(0, ROPE_DIM, 2, dtype=f32) / ROPE_DIM)
    ang = pos.astype(f32)[:, None] * inv_freq[None, :]
    cos, sin = jnp.cos(ang), jnp.sin(ang)
    lane = np.arange(LANES) % DIFF_DH
    sel = lane % ROPE_HALF
    lo = jnp.asarray(lane < ROPE_HALF)[None, :]
    hi = jnp.asarray((lane >= ROPE_HALF) & (lane < ROPE_DIM))[None, :]
    cos_l, sin_l = cos[:, sel], sin[:, sel]
    cn = jnp.where(lo | hi, cos_l, 1.0)
    s1n = jnp.where(lo, -sin_l, 0.0)
    s2n = jnp.where(hi, sin_l, 0.0)
    return cn, s1n, s2n, cos.T, sin.T


def kernel(x, norm1_g, w_in, b_f, lam_q1, lam_k1, lam_q2, lam_k2, subln_g, w_out, norm2_g, w_gate, w_up,
           w_down, normf_g):
    f32, bf16 = jnp.float32, jnp.bfloat16
    B, S, D = x.shape
    en, pt, onesn, tri = _placement_tables()
    cn, s1n, s2n, ct, st = _rope_tables(S)

    w = w_in[0]
    o_dq, o_dk, o_dv, o_fq, o_fk, o_fv, o_fg = 0, 512, 1024, 1536, 2048, 2560, 3072
    pad_heads = lambda m: jnp.pad(m.reshape(D, FOX_HEADS, FOX_DH), ((0, 0), (0, 0), (0, LANES - FOX_DH))
                                  ).reshape(D, FOX_HEADS * LANES)
    wnat = jnp.concatenate(
        [w[:, o_dk:o_dv], pad_heads(w[:, o_fk:o_fv]), jnp.pad(w[:, o_fg:], ((0, 0), (0, LANES - FOX_HEADS)))],
        axis=1).astype(bf16)
    wtr = jnp.concatenate(
        [w[:, o_dq:o_dk], w[:, o_dv:o_fq], pad_heads(w[:, o_fq:o_fk]), w[:, o_fv:o_fg],
         jnp.pad(w[:, o_fg:], ((0, 0), (0, TR_FG - FOX_HEADS)))], axis=1).T.astype(bf16)
    bfn = jnp.pad(b_f[0].astype(f32), (0, LANES - FOX_HEADS))[None, :]
    bft = jnp.broadcast_to(jnp.pad(b_f[0].astype(f32), (0, TR_FG - FOX_HEADS))[:, None], (TR_FG, PROJ_TM))

    kd, kf, qd, vd, qf, vf = _proj_call(
        x, norm1_g[0][None, :], wnat, wtr, bfn, bft, cn, s1n, s2n, ct, st,
        jnp.asarray(tri, bf16), jnp.asarray(en, bf16), jnp.asarray(pt, bf16), jnp.asarray(onesn, f32))

    gcol = jnp.broadcast_to(subln_g[0].astype(f32)[:, None], (DIFF_VD, ATT_TQ))
    row = lambda v: v[0].astype(f32)[None, :]
    tab_d, masks_d = _sweep_tables(S, CHUNK)
    tab_f, masks_f = _sweep_tables(S, 1)
    mixd = _dattn_call(jnp.asarray(tab_d), jnp.asarray(masks_d), qd, kd, vd,
                       row(lam_q1), row(lam_k1), row(lam_q2), row(lam_k2), gcol)
    mixf = _fattn_call(jnp.asarray(tab_f), jnp.asarray(masks_f), qf, kf, vf)

    wo = w_out[0].astype(bf16)
    half = DIFF_HEADS * DIFF_VD
    out = _ffn_call(
        x.reshape(B * S, D), mixd.reshape(B * S, half), mixf.reshape(B * S, half),
        wo[:half], wo[half:], norm2_g[0][None, :], w_gate[0].astype(bf16), w_up[0].astype(bf16),
        w_down[0].astype(bf16), normf_g[None, :])
    return out.reshape(B, S, D)
```

```python
import math

import jax
import jax.numpy as jnp
import numpy as np
from jax import lax
from jax.experimental import pallas as pl
from jax.experimental.pallas import tpu as pltpu

D_MODEL = 1024
CHUNK = 64
DIFF_HEADS = 4
DIFF_DH = 64
DIFF_VD = 128
FOX_HEADS = 8
FOX_DH = 64
ROPE_THETA = 500000.0
ROPE_DIM = 16
ROPE_HALF = 8
D_FF = 2816
EPS = 1e-5
NEG = -1e30
LAM_INIT = 0.8 - 0.6 * math.exp(-0.3 * 0)
LOG2E = 1.4426950408889634

LANES = 128
MXU_N = 256
VMEM_LIMIT_BYTES = 56 * 1024 * 1024

PROJ_TM = 512
ATT_TQ = MXU_N
ATT_TK = 512
ATT_GROUP = 2
FFN_TM = 512

NAT_DK = DIFF_HEADS * 2 * DIFF_DH
NAT_FK = FOX_HEADS * LANES
NAT_W = NAT_DK + NAT_FK + LANES
TR_DQ = DIFF_HEADS * 2 * DIFF_DH
TR_DV = DIFF_HEADS * DIFF_VD
TR_FQ = FOX_HEADS * LANES
TR_FV = FOX_HEADS * FOX_DH
TR_FG = 16
TR_W = TR_DQ + TR_DV + TR_FQ + TR_FV + TR_FG
AUG_ONES = FOX_DH
AUG_BIAS = FOX_DH + 3


def _nt_dot(a, b):
    return lax.dot_general(a, b, (((1,), (1,)), ((), ())), preferred_element_type=jnp.float32)


def _split3(x):
    hi = x.astype(jnp.bfloat16)
    r = x - hi.astype(jnp.float32)
    mid = r.astype(jnp.bfloat16)
    lo = (r - mid.astype(jnp.float32)).astype(jnp.bfloat16)
    return hi, mid, lo


def _log_sigmoid(x):
    return jnp.minimum(x, 0.0) - jnp.log1p(jnp.exp(-jnp.abs(x)))


def _proj_kernel(x_ref, g_ref, wnat_ref, wtr_ref, bfn_ref, bft_ref, cn_ref, s1n_ref, s2n_ref,
                 ct_ref, st_ref, tri_ref, en_ref, pt_ref, onesn_ref,
                 kd_ref, kf_ref, qd_ref, vd_ref, qf_ref, vf_ref,
                 carry_n, carry_t):
    @pl.when(pl.program_id(1) == 0)
    def _():
        carry_n[...] = jnp.zeros_like(carry_n)
        carry_t[...] = jnp.zeros_like(carry_t)

    tm = x_ref.shape[1]
    x = x_ref[0]
    ms = jnp.mean(x * x, axis=-1, keepdims=True)
    h = (x * lax.rsqrt(ms + EPS) * g_ref[...]).astype(jnp.bfloat16)

    nat = jnp.dot(h, wnat_ref[...], preferred_element_type=jnp.float32)
    tr = _nt_dot(wtr_ref[...], h)

    cn, s1n, s2n = cn_ref[...], s1n_ref[...], s2n_ref[...]
    for hd in range(DIFF_HEADS):
        t = nat[:, hd * LANES:(hd + 1) * LANES]
        rot = t * cn + pltpu.roll(t, LANES - ROPE_HALF, 1) * s1n + pltpu.roll(t, ROPE_HALF, 1) * s2n
        kd_ref[0, hd] = rot.astype(jnp.bfloat16)

    tri = tri_ref[...]

    lf_n = _log_sigmoid(nat[:, NAT_DK + NAT_FK:] + bfn_ref[...])
    c3 = jnp.dot(tri, jnp.concatenate(_split3(lf_n), axis=1), preferred_element_type=jnp.float32)
    cum_n = (c3[:, 2 * LANES:] + c3[:, LANES:2 * LANES]) + c3[:, :LANES] + carry_n[0:1, :]
    carry_n[...] = jnp.broadcast_to(cum_n[tm - 1:tm, :], carry_n.shape)
    kb = jnp.concatenate(_split3(cum_n * (-LOG2E)), axis=1)
    aug_n = jnp.dot(kb, en_ref[...], preferred_element_type=jnp.float32)
    kf = nat[:, NAT_DK:NAT_DK + NAT_FK] + aug_n + onesn_ref[...]
    for hd in range(FOX_HEADS):
        kf_ref[0, hd] = kf[:, hd * LANES:(hd + 1) * LANES].astype(jnp.bfloat16)

    ct, st = ct_ref[...], st_ref[...]
    pieces = []
    for comp in range(DIFF_HEADS * 2):
        base = comp * DIFF_DH
        t1 = tr[base:base + ROPE_HALF]
        t2 = tr[base + ROPE_HALF:base + ROPE_DIM]
        pieces += [t1 * ct - t2 * st, t2 * ct + t1 * st, tr[base + ROPE_DIM:base + DIFF_DH]]
    qd = jnp.concatenate(pieces, axis=0) * (DIFF_DH ** -0.5 * LOG2E)
    qd_ref[0] = qd.astype(jnp.bfloat16)
    vd_ref[0] = tr[TR_DQ:TR_DQ + TR_DV].astype(jnp.bfloat16)

    o_fq = TR_DQ + TR_DV
    o_fv = o_fq + TR_FQ
    o_fg = o_fv + TR_FV
    lf_t = _log_sigmoid(tr[o_fg:o_fg + TR_FG] + bft_ref[...])
    c3t = _nt_dot(jnp.concatenate(_split3(lf_t), axis=0), tri)
    cum_t = (c3t[2 * TR_FG:] + c3t[TR_FG:2 * TR_FG]) + c3t[:TR_FG] + carry_t[:, 0:1]
    carry_t[...] = jnp.broadcast_to(cum_t[:, tm - 1:tm], carry_t.shape)
    qb = jnp.concatenate(_split3(cum_t * LOG2E) + (jnp.ones((TR_FG, tm), jnp.bfloat16),), axis=0)
    aug_t = jnp.dot(pt_ref[...], qb, preferred_element_type=jnp.float32)
    qf = tr[o_fq:o_fq + TR_FQ] * (FOX_DH ** -0.5 * LOG2E) + aug_t
    qf_ref[0] = qf.astype(jnp.bfloat16)
    vf_ref[0] = tr[o_fv:o_fv + TR_FV].astype(jnp.bfloat16)


def _proj_call(x, g1, wnat, wtr, bfn, bft, cn, s1n, s2n, ct, st, tri, en, pt, onesn):
    B, S, D = x.shape
    tm = PROJ_TM
    const = lambda shape: pl.BlockSpec(shape, lambda b, i: (0,) * len(shape))
    bf16 = jnp.bfloat16
    return pl.pallas_call(
        _proj_kernel,
        grid=(B, S // tm),
        in_specs=[
            pl.BlockSpec((1, tm, D), lambda b, i: (b, i, 0)),
            const((1, D)),
            const((D, NAT_W)),
            const((TR_W, D)),
            const((1, LANES)),
            const((TR_FG, tm)),
            pl.BlockSpec((tm, LANES), lambda b, i: (i, 0)),
            pl.BlockSpec((tm, LANES), lambda b, i: (i, 0)),
            pl.BlockSpec((tm, LANES), lambda b, i: (i, 0)),
            pl.BlockSpec((ROPE_HALF, tm), lambda b, i: (0, i)),
            pl.BlockSpec((ROPE_HALF, tm), lambda b, i: (0, i)),
            const((tm, tm)),
            const((3 * LANES, NAT_FK)),
            const((TR_FQ, 4 * TR_FG)),
            const((1, NAT_FK)),
        ],
        out_specs=[
            pl.BlockSpec((1, DIFF_HEADS, tm, LANES), lambda b, i: (b, 0, i, 0)),
            pl.BlockSpec((1, FOX_HEADS, tm, LANES), lambda b, i: (b, 0, i, 0)),
            pl.BlockSpec((1, TR_DQ, tm), lambda b, i: (b, 0, i)),
            pl.BlockSpec((1, TR_DV, tm), lambda b, i: (b, 0, i)),
            pl.BlockSpec((1, TR_FQ, tm), lambda b, i: (b, 0, i)),
            pl.BlockSpec((1, TR_FV, tm), lambda b, i: (b, 0, i)),
        ],
        out_shape=[
            jax.ShapeDtypeStruct((B, DIFF_HEADS, S, LANES), bf16),
            jax.ShapeDtypeStruct((B, FOX_HEADS, S, LANES), bf16),
            jax.ShapeDtypeStruct((B, TR_DQ, S), bf16),
            jax.ShapeDtypeStruct((B, TR_DV, S), bf16),
            jax.ShapeDtypeStruct((B, TR_FQ, S), bf16),
            jax.ShapeDtypeStruct((B, TR_FV, S), bf16),
        ],
        scratch_shapes=[pltpu.VMEM((8, LANES), jnp.float32), pltpu.VMEM((TR_FG, LANES), jnp.float32)],
        compiler_params=pltpu.CompilerParams(
            dimension_semantics=("arbitrary", "arbitrary"), vmem_limit_bytes=VMEM_LIMIT_BYTES),
        name="proj",
    )(x, g1, wnat, wtr, bfn, bft, cn, s1n, s2n, ct, st, tri, en, pt, onesn)


def _sweep_tables(S, chunk):
    nq, ratio = S // ATT_TQ, ATT_TK // ATT_TQ
    diag = [(i, i // ratio, i % ratio) for i in range(nq)]
    full = [(i, j, 0) for i in range(nq) for j in range(i // ratio)]
    assert len(full) % (2 * ATT_GROUP) == 0 and len(diag) % (2 * ATT_GROUP) == 0
    kk = np.arange(ATT_TK)[:, None] // chunk
    masks = []
    for r in range(ratio):
        qq = (r * ATT_TQ + np.arange(ATT_TQ))[None, :] // chunk
        masks.append(np.where(kk <= qq, 0.0, NEG).astype(np.float32))
    as_tab = lambda pairs: np.array(pairs, np.int32).T.copy()
    return as_tab(full), as_tab(diag), np.stack(masks)


def _init_sweep_state(m_s, l_s, acc_s):
    m_s[...] = jnp.full_like(m_s, NEG)
    l_s[...] = jnp.zeros_like(l_s)
    acc_s[...] = jnp.zeros_like(acc_s)


def _flat_sweep(tab_ref, mask_ref, n_heads, scores, values, m_s, l_s, acc_s, sbuf, xbuf, pbuf, abuf):
    n_groups = tab_ref.shape[1] // ATT_GROUP
    assert n_groups >= 2 and n_groups % 2 == 0

    def slot(par, a):
        return par * ATT_GROUP + a

    def stage_scores(u, par):
        for a in range(ATT_GROUP):
            t = u * ATT_GROUP + a
            it, jt = tab_ref[0, t], tab_ref[1, t]
            for h in range(n_heads):
                s = scores(h, it, jt)
                if mask_ref is not None:
                    s = s + mask_ref[tab_ref[2, t]]
                sbuf[slot(par, a), h] = s
                xbuf[slot(par, a), h] = jnp.max(s, axis=0, keepdims=True)

    def stage_softmax(u, par):
        for a in range(ATT_GROUP):
            it = tab_ref[0, u * ATT_GROUP + a]
            for h in range(n_heads):
                m_old = m_s[h, it]
                m_new = jnp.maximum(m_old, xbuf[slot(par, a), h])
                alpha = jnp.exp2(m_old - m_new)
                p = jnp.exp2(sbuf[slot(par, a), h] - m_new)
                l_s[h, it] = alpha * l_s[h, it] + jnp.sum(p, axis=0, keepdims=True)
                m_s[h, it] = m_new
                pbuf[slot(par, a), h] = p.astype(jnp.bfloat16)
                abuf[slot(par, a), h] = alpha

    def stage_values(u, par):
        for a in range(ATT_GROUP):
            t = u * ATT_GROUP + a
            it, jt = tab_ref[0, t], tab_ref[1, t]
            for h in range(n_heads):
                pv = jnp.dot(values(h, jt), pbuf[slot(par, a), h], preferred_element_type=jnp.float32)
                acc_s[h, it] = abuf[slot(par, a), h] * acc_s[h, it] + pv

    def step(u, par, with_values=True):
        stage_scores(u, par)
        if with_values:
            stage_values(u - 2, par)
        stage_softmax(u - 1, 1 - par)

    stage_scores(0, 0)
    step(1, 1, with_values=False)

    def body(v, c):
        step(2 * v, 0)
        step(2 * v + 1, 1)
        return c

    lax.fori_loop(1, n_groups // 2, body, 0)
    stage_values(n_groups - 2, 0)
    stage_softmax(n_groups - 1, 1)
    stage_values(n_groups - 1, 1)


def _sweep_scratch(n_heads, nq, dv):
    f32 = jnp.float32
    slots = 2 * ATT_GROUP
    return [
        pltpu.VMEM((n_heads, nq, 1, ATT_TQ), f32),
        pltpu.VMEM((n_heads, nq, 1, ATT_TQ), f32),
        pltpu.VMEM((n_heads, nq, dv, ATT_TQ), f32),
        pltpu.VMEM((slots, n_heads, ATT_TK, ATT_TQ), f32),
        pltpu.VMEM((slots, n_heads, 1, ATT_TQ), f32),
        pltpu.VMEM((slots, n_heads, ATT_TK, ATT_TQ), jnp.bfloat16),
        pltpu.VMEM((slots, n_heads, 1, ATT_TQ), f32),
    ]


def _key_slice(jt):
    return pl.ds(pl.multiple_of(jt * ATT_TK, ATT_TK), ATT_TK)


def _query_slice(it):
    return pl.ds(pl.multiple_of(it * ATT_TQ, ATT_TQ), ATT_TQ)


def _dattn_kernel(full_ref, diag_ref, q_ref, k_ref, v_ref, mask_ref, lq1_ref, lk1_ref, lq2_ref, lk2_ref, g_ref,
                  o_ref, m_s, l_s, acc_s, sbuf, xbuf, pbuf, abuf):
    nq = m_s.shape[1]
    row = lax.broadcasted_iota(jnp.int32, (2 * DIFF_DH, ATT_TQ), 0)

    def scores(h, it, jt):
        q = q_ref[0, :, _query_slice(it)]
        keep = (row < DIFF_DH) if h == 0 else (row >= DIFF_DH)
        rhs = jnp.where(keep, q, jnp.zeros_like(q))
        return jnp.dot(k_ref[0, 0, _key_slice(jt), :], rhs, preferred_element_type=jnp.float32)

    def values(h, jt):
        return v_ref[0, :, _key_slice(jt)]

    _init_sweep_state(m_s, l_s, acc_s)
    _flat_sweep(full_ref, None, 2, scores, values, m_s, l_s, acc_s, sbuf, xbuf, pbuf, abuf)
    _flat_sweep(diag_ref, mask_ref, 2, scores, values, m_s, l_s, acc_s, sbuf, xbuf, pbuf, abuf)

    lam = (jnp.exp(jnp.sum(lq1_ref[...] * lk1_ref[...], axis=-1, keepdims=True))
           - jnp.exp(jnp.sum(lq2_ref[...] * lk2_ref[...], axis=-1, keepdims=True)) + LAM_INIT)
    gain = g_ref[...] * (1.0 - LAM_INIT)

    def finalize(it, c):
        o = acc_s[0, it] / l_s[0, it] - lam * (acc_s[1, it] / l_s[1, it])
        ms = jnp.mean(o * o, axis=0, keepdims=True)
        o = o * lax.rsqrt(ms + EPS) * gain
        o_ref[0, _query_slice(it), :] = o.T.astype(o_ref.dtype)
        return c

    lax.fori_loop(0, nq, finalize, 0)


def _dattn_call(full, diag, masks, qd, kd, vd, lq1, lk1, lq2, lk2, gcol):
    B, _, S = qd.shape
    nq = S // ATT_TQ
    vec = lambda: pl.BlockSpec((1, DIFF_DH), lambda b, h, *_: (0, 0))
    return pl.pallas_call(
        _dattn_kernel,
        grid_spec=pltpu.PrefetchScalarGridSpec(
            num_scalar_prefetch=2,
            grid=(B, DIFF_HEADS),
            in_specs=[
                pl.BlockSpec((1, 2 * DIFF_DH, S), lambda b, h, *_: (b, h, 0)),
                pl.BlockSpec((1, 1, S, LANES), lambda b, h, *_: (b, h, 0, 0)),
                pl.BlockSpec((1, DIFF_VD, S), lambda b, h, *_: (b, h, 0)),
                pl.BlockSpec(masks.shape, lambda b, h, *_: (0, 0, 0)),
                vec(), vec(), vec(), vec(),
                pl.BlockSpec((DIFF_VD, ATT_TQ), lambda b, h, *_: (0, 0)),
            ],
            out_specs=pl.BlockSpec((1, S, DIFF_VD), lambda b, h, *_: (b, 0, h)),
            scratch_shapes=_sweep_scratch(2, nq, DIFF_VD),
        ),
        out_shape=jax.ShapeDtypeStruct((B, S, DIFF_HEADS * DIFF_VD), jnp.bfloat16),
        compiler_params=pltpu.CompilerParams(
            dimension_semantics=("arbitrary", "arbitrary"), vmem_limit_bytes=VMEM_LIMIT_BYTES),
        name="dattn",
    )(full, diag, qd, kd, vd, masks, lq1, lk1, lq2, lk2, gcol)


def _fattn_kernel(full_ref, diag_ref, q_ref, k_ref, v_ref, mask_ref, o_ref, m_s, l_s, acc_s, sbuf, xbuf, pbuf, abuf):
    nq = m_s.shape[1]

    def scores(h, it, jt):
        q = q_ref[0, h * LANES:(h + 1) * LANES, _query_slice(it)]
        return jnp.dot(k_ref[0, h, _key_slice(jt), :], q, preferred_element_type=jnp.float32)

    def values(h, jt):
        return v_ref[0, h * FOX_DH:(h + 1) * FOX_DH, _key_slice(jt)]

    _init_sweep_state(m_s, l_s, acc_s)
    _flat_sweep(full_ref, None, 2, scores, values, m_s, l_s, acc_s, sbuf, xbuf, pbuf, abuf)
    _flat_sweep(diag_ref, mask_ref, 2, scores, values, m_s, l_s, acc_s, sbuf, xbuf, pbuf, abuf)

    def finalize(it, c):
        o = jnp.concatenate([acc_s[0, it] / l_s[0, it], acc_s[1, it] / l_s[1, it]], axis=0)
        o_ref[0, _query_slice(it), :] = o.T.astype(o_ref.dtype)
        return c

    lax.fori_loop(0, nq, finalize, 0)


def _fattn_call(full, diag, masks, qf, kf, vf):
    B, _, S = qf.shape
    nq = S // ATT_TQ
    return pl.pallas_call(
        _fattn_kernel,
        grid_spec=pltpu.PrefetchScalarGridSpec(
            num_scalar_prefetch=2,
            grid=(B, FOX_HEADS // 2),
            in_specs=[
                pl.BlockSpec((1, 2 * LANES, S), lambda b, p, *_: (b, p, 0)),
                pl.BlockSpec((1, 2, S, LANES), lambda b, p, *_: (b, p, 0, 0)),
                pl.BlockSpec((1, 2 * FOX_DH, S), lambda b, p, *_: (b, p, 0)),
                pl.BlockSpec(masks.shape, lambda b, p, *_: (0, 0, 0)),
            ],
            out_specs=pl.BlockSpec((1, S, 2 * FOX_DH), lambda b, p, *_: (b, 0, p)),
            scratch_shapes=_sweep_scratch(2, nq, FOX_DH),
        ),
        out_shape=jax.ShapeDtypeStruct((B, S, FOX_HEADS * FOX_DH), jnp.bfloat16),
        compiler_params=pltpu.CompilerParams(
            dimension_semantics=("arbitrary", "arbitrary"), vmem_limit_bytes=VMEM_LIMIT_BYTES),
        name="fattn",
    )(full, diag, qf, kf, vf, masks)


def _ffn_kernel(x_ref, md_ref, mf_ref, wod_ref, wof_ref, g2_ref, wg_ref, wu_ref, wd_ref, gf_ref, o_ref):
    x1 = (x_ref[...]
          + jnp.dot(md_ref[...], wod_ref[...], preferred_element_type=jnp.float32)
          + jnp.dot(mf_ref[...], wof_ref[...], preferred_element_type=jnp.float32))
    ms = jnp.mean(x1 * x1, axis=-1, keepdims=True)
    h2 = (x1 * lax.rsqrt(ms + EPS) * g2_ref[...]).astype(jnp.bfloat16)
    g = jnp.dot(h2, wg_ref[...], preferred_element_type=jnp.float32)
    u = jnp.dot(h2, wu_ref[...], preferred_element_type=jnp.float32)
    act = (g * jax.nn.sigmoid(g) * u).astype(jnp.bfloat16)
    y = x1 + jnp.dot(act, wd_ref[...], preferred_element_type=jnp.float32)
    ms = jnp.mean(y * y, axis=-1, keepdims=True)
    o_ref[...] = y * lax.rsqrt(ms + EPS) * gf_ref[...]


def _ffn_call(x2, md, mf, wod, wof, g2, wg, wu, wd, gf):
    n, D = x2.shape
    tm = FFN_TM
    half = md.shape[1]
    resident = lambda shape: pl.BlockSpec(shape, lambda i: (0, 0), pipeline_mode=pl.Buffered(1))
    return pl.pallas_call(
        _ffn_kernel,
        grid=(n // tm,),
        in_specs=[
            pl.BlockSpec((tm, D), lambda i: (i, 0)),
            pl.BlockSpec((tm, half), lambda i: (i, 0)),
            pl.BlockSpec((tm, half), lambda i: (i, 0)),
            resident((half, D)),
            resident((half, D)),
            resident((1, D)),
            resident((D, D_FF)),
            resident((D, D_FF)),
            resident((D_FF, D)),
            resident((1, D)),
        ],
        out_specs=pl.BlockSpec((tm, D), lambda i: (i, 0)),
        out_shape=jax.ShapeDtypeStruct((n, D), jnp.float32),
        compiler_params=pltpu.CompilerParams(
            dimension_semantics=("arbitrary",), vmem_limit_bytes=VMEM_LIMIT_BYTES),
        name="ffn",
    )(x2, md, mf, wod, wof, g2, wg, wu, wd, gf)


def _placement_tables():
    en = np.zeros((3 * LANES, NAT_FK), np.float32)
    pt = np.zeros((TR_FQ, 4 * TR_FG), np.float32)
    onesn = np.zeros((1, NAT_FK), np.float32)
    for h in range(FOX_HEADS):
        for p in range(3):
            en[p * LANES + h, h * LANES + AUG_BIAS + p] = 1.0
            onesn[0, h * LANES + AUG_ONES + p] = 1.0
            pt[h * LANES + AUG_ONES + p, p * TR_FG + h] = 1.0
            pt[h * LANES + AUG_BIAS + p, 3 * TR_FG] = 1.0
    tri = np.tril(np.ones((PROJ_TM, PROJ_TM), np.float32))
    return en, pt, onesn, tri


def _rope_tables(S):
    f32 = jnp.float32
    pos = jnp.arange(S, dtype=jnp.int32)
    inv_freq = ROPE_THETA ** (-jnp.arange(0, ROPE_DIM, 2, dtype=f32) / ROPE_DIM)
    ang = pos.astype(f32)[:, None] * inv_freq[None, :]
    cos, sin = jnp.cos(ang), jnp.sin(ang)
    lane = np.arange(LANES) % DIFF_DH
    sel = lane % ROPE_HALF
    lo = jnp.asarray(lane < ROPE_HALF)[None, :]
    hi = jnp.asarray((lane >= ROPE_HALF) & (lane < ROPE_DIM))[None, :]
    cos_l, sin_l = cos[:, sel], sin[:, sel]
    cn = jnp.where(lo | hi, cos_l, 1.0)
    s1n = jnp.where(lo, -sin_l, 0.0)
    s2n = jnp.where(hi, sin_l, 0.0)
    return cn, s1n, s2n, cos.T, sin.T


def kernel(x, norm1_g, w_in, b_f, lam_q1, lam_k1, lam_q2, lam_k2, subln_g, w_out, norm2_g, w_gate, w_up,
           w_down, normf_g):
    f32, bf16 = jnp.float32, jnp.bfloat16
    B, S, D = x.shape
    en, pt, onesn, tri = _placement_tables()
    cn, s1n, s2n, ct, st = _rope_tables(S)

    w = w_in[0]
    o_dq, o_dk, o_dv, o_fq, o_fk, o_fv, o_fg = 0, 512, 1024, 1536, 2048, 2560, 3072
    pad_heads = lambda m: jnp.pad(m.reshape(D, FOX_HEADS, FOX_DH), ((0, 0), (0, 0), (0, LANES - FOX_DH))
                                  ).reshape(D, FOX_HEADS * LANES)
    wnat = jnp.concatenate(
        [w[:, o_dk:o_dv], pad_heads(w[:, o_fk:o_fv]), jnp.pad(w[:, o_fg:], ((0, 0), (0, LANES - FOX_HEADS)))],
        axis=1).astype(bf16)
    wtr = jnp.concatenate(
        [w[:, o_dq:o_dk], w[:, o_dv:o_fq], pad_heads(w[:, o_fq:o_fk]), w[:, o_fv:o_fg],
         jnp.pad(w[:, o_fg:], ((0, 0), (0, TR_FG - FOX_HEADS)))], axis=1).T.astype(bf16)
    bfn = jnp.pad(b_f[0].astype(f32), (0, LANES - FOX_HEADS))[None, :]
    bft = jnp.broadcast_to(jnp.pad(b_f[0].astype(f32), (0, TR_FG - FOX_HEADS))[:, None], (TR_FG, PROJ_TM))

    kd, kf, qd, vd, qf, vf = _proj_call(
        x, norm1_g[0][None, :], wnat, wtr, bfn, bft, cn, s1n, s2n, ct, st,
        jnp.asarray(tri, bf16), jnp.asarray(en, bf16), jnp.asarray(pt, bf16), jnp.asarray(onesn, f32))

    gcol = jnp.broadcast_to(subln_g[0].astype(f32)[:, None], (DIFF_VD, ATT_TQ))
    row = lambda v: v[0].astype(f32)[None, :]
    full, diag, masks_d = _sweep_tables(S, CHUNK)
    _, _, masks_f = _sweep_tables(S, 1)
    full, diag = jnp.asarray(full), jnp.asarray(diag)
    mixd = _dattn_call(full, diag, jnp.asarray(masks_d), qd, kd, vd,
                       row(lam_q1), row(lam_k1), row(lam_q2), row(lam_k2), gcol)
    mixf = _fattn_call(full, diag, jnp.asarray(masks_f), qf, kf, vf)

    wo = w_out[0].astype(bf16)
    half = DIFF_HEADS * DIFF_VD
    out = _ffn_call(
        x.reshape(B * S, D), mixd.reshape(B * S, half), mixf.reshape(B * S, half),
        wo[:half], wo[half:], norm2_g[0][None, :], w_gate[0].astype(bf16), w_up[0].astype(bf16),
        w_down[0].astype(bf16), normf_g[None, :])
    return out.reshape(B, S, D)
```

```python
import math

import jax
import jax.numpy as jnp
import numpy as np
from jax import lax
from jax.experimental import pallas as pl
from jax.experimental.pallas import tpu as pltpu

D_MODEL = 1024
CHUNK = 64
DIFF_HEADS = 4
DIFF_DH = 64
DIFF_VD = 128
FOX_HEADS = 8
FOX_DH = 64
ROPE_THETA = 500000.0
ROPE_DIM = 16
ROPE_HALF = 8
D_FF = 2816
EPS = 1e-5
NEG = -1e30
LAM_INIT = 0.8 - 0.6 * math.exp(-0.3 * 0)
LOG2E = 1.4426950408889634

LANES = 128
MXU_N = 256
VMEM_LIMIT_BYTES = 56 * 1024 * 1024

PROJ_TM = 512
ATT_TQ = MXU_N
ATT_TK = 512
ATT_GROUP = 2
ATT_ONES_ROWS = 16
FFN_TM = 512

NAT_DK = DIFF_HEADS * 2 * DIFF_DH
NAT_FK = FOX_HEADS * LANES
NAT_W = NAT_DK + NAT_FK + LANES
TR_DQ = DIFF_HEADS * 2 * DIFF_DH
TR_DV = DIFF_HEADS * DIFF_VD
TR_FQ = FOX_HEADS * LANES
TR_FV = FOX_HEADS * FOX_DH
TR_FG = 16
TR_W = TR_DQ + TR_DV + TR_FQ + TR_FV + TR_FG
AUG_ONES = FOX_DH
AUG_BIAS = FOX_DH + 3


def _nt_dot(a, b):
    return lax.dot_general(a, b, (((1,), (1,)), ((), ())), preferred_element_type=jnp.float32)


def _split3(x):
    hi = x.astype(jnp.bfloat16)
    r = x - hi.astype(jnp.float32)
    mid = r.astype(jnp.bfloat16)
    lo = (r - mid.astype(jnp.float32)).astype(jnp.bfloat16)
    return hi, mid, lo


def _log_sigmoid(x):
    return jnp.minimum(x, 0.0) - jnp.log1p(jnp.exp(-jnp.abs(x)))


def _store_tiles(ref, x, rows, cols):
    for hd in range(x.shape[0] // rows):
        for c in range(x.shape[1] // cols):
            ref[0, hd, c] = x[hd * rows:(hd + 1) * rows, c * cols:(c + 1) * cols]


def _proj_kernel(x_ref, g_ref, wnat_ref, wtr_ref, bfn_ref, bft_ref, cn_ref, s1n_ref, s2n_ref,
                 ct_ref, st_ref, tri_ref, en_ref, pt_ref, onesn_ref,
                 kd_ref, kf_ref, qd_ref, vd_ref, qf_ref, vf_ref,
                 carry_n, carry_t):
    @pl.when(pl.program_id(1) == 0)
    def _():
        carry_n[...] = jnp.zeros_like(carry_n)
        carry_t[...] = jnp.zeros_like(carry_t)

    tm = x_ref.shape[1]
    x = x_ref[0]
    ms = jnp.mean(x * x, axis=-1, keepdims=True)
    h = (x * lax.rsqrt(ms + EPS) * g_ref[...]).astype(jnp.bfloat16)

    nat = jnp.dot(h, wnat_ref[...], preferred_element_type=jnp.float32)
    tr = _nt_dot(wtr_ref[...], h)

    cn, s1n, s2n = cn_ref[...], s1n_ref[...], s2n_ref[...]
    for hd in range(DIFF_HEADS):
        t = nat[:, hd * LANES:(hd + 1) * LANES]
        rot = t * cn + pltpu.roll(t, LANES - ROPE_HALF, 1) * s1n + pltpu.roll(t, ROPE_HALF, 1) * s2n
        kd_ref[0, hd] = rot.astype(jnp.bfloat16)

    tri = tri_ref[...]

    lf_n = _log_sigmoid(nat[:, NAT_DK + NAT_FK:] + bfn_ref[...])
    c3 = jnp.dot(tri, jnp.concatenate(_split3(lf_n), axis=1), preferred_element_type=jnp.float32)
    cum_n = (c3[:, 2 * LANES:] + c3[:, LANES:2 * LANES]) + c3[:, :LANES] + carry_n[0:1, :]
    carry_n[...] = jnp.broadcast_to(cum_n[tm - 1:tm, :], carry_n.shape)
    kb = jnp.concatenate(_split3(cum_n * (-LOG2E)), axis=1)
    aug_n = jnp.dot(kb, en_ref[...], preferred_element_type=jnp.float32)
    kf = nat[:, NAT_DK:NAT_DK + NAT_FK] + aug_n + onesn_ref[...]
    for hd in range(FOX_HEADS):
        kf_ref[0, hd] = kf[:, hd * LANES:(hd + 1) * LANES].astype(jnp.bfloat16)

    ct, st = ct_ref[...], st_ref[...]
    pieces = []
    for comp in range(DIFF_HEADS * 2):
        base = comp * DIFF_DH
        t1 = tr[base:base + ROPE_HALF]
        t2 = tr[base + ROPE_HALF:base + ROPE_DIM]
        pieces += [t1 * ct - t2 * st, t2 * ct + t1 * st, tr[base + ROPE_DIM:base + DIFF_DH]]
    qd = jnp.concatenate(pieces, axis=0) * (DIFF_DH ** -0.5 * LOG2E)
    _store_tiles(qd_ref, qd.astype(jnp.bfloat16), 2 * DIFF_DH, ATT_TQ)
    _store_tiles(vd_ref, tr[TR_DQ:TR_DQ + TR_DV].astype(jnp.bfloat16), DIFF_VD, ATT_TK)

    o_fq = TR_DQ + TR_DV
    o_fv = o_fq + TR_FQ
    o_fg = o_fv + TR_FV
    lf_t = _log_sigmoid(tr[o_fg:o_fg + TR_FG] + bft_ref[...])
    c3t = _nt_dot(jnp.concatenate(_split3(lf_t), axis=0), tri)
    cum_t = (c3t[2 * TR_FG:] + c3t[TR_FG:2 * TR_FG]) + c3t[:TR_FG] + carry_t[:, 0:1]
    carry_t[...] = jnp.broadcast_to(cum_t[:, tm - 1:tm], carry_t.shape)
    qb = jnp.concatenate(_split3(cum_t * LOG2E) + (jnp.ones((TR_FG, tm), jnp.bfloat16),), axis=0)
    aug_t = jnp.dot(pt_ref[...], qb, preferred_element_type=jnp.float32)
    qf = tr[o_fq:o_fq + TR_FQ] * (FOX_DH ** -0.5 * LOG2E) + aug_t
    _store_tiles(qf_ref, qf.astype(jnp.bfloat16), LANES, ATT_TQ)
    _store_tiles(vf_ref, tr[o_fv:o_fv + TR_FV].astype(jnp.bfloat16), FOX_DH, ATT_TK)


def _proj_call(x, g1, wnat, wtr, bfn, bft, cn, s1n, s2n, ct, st, tri, en, pt, onesn):
    B, S, D = x.shape
    tm = PROJ_TM
    const = lambda shape: pl.BlockSpec(shape, lambda b, i: (0,) * len(shape))
    tiles = lambda heads, rows, cols: pl.BlockSpec((1, heads, tm // cols, rows, cols), lambda b, i: (b, 0, i, 0, 0))
    bf16 = jnp.bfloat16
    return pl.pallas_call(
        _proj_kernel,
        grid=(B, S // tm),
        in_specs=[
            pl.BlockSpec((1, tm, D), lambda b, i: (b, i, 0)),
            const((1, D)),
            const((D, NAT_W)),
            const((TR_W, D)),
            const((1, LANES)),
            const((TR_FG, tm)),
            pl.BlockSpec((tm, LANES), lambda b, i: (i, 0)),
            pl.BlockSpec((tm, LANES), lambda b, i: (i, 0)),
            pl.BlockSpec((tm, LANES), lambda b, i: (i, 0)),
            pl.BlockSpec((ROPE_HALF, tm), lambda b, i: (0, i)),
            pl.BlockSpec((ROPE_HALF, tm), lambda b, i: (0, i)),
            const((tm, tm)),
            const((3 * LANES, NAT_FK)),
            const((TR_FQ, 4 * TR_FG)),
            const((1, NAT_FK)),
        ],
        out_specs=[
            pl.BlockSpec((1, DIFF_HEADS, tm, LANES), lambda b, i: (b, 0, i, 0)),
            pl.BlockSpec((1, FOX_HEADS, tm, LANES), lambda b, i: (b, 0, i, 0)),
            tiles(DIFF_HEADS, 2 * DIFF_DH, ATT_TQ),
            tiles(DIFF_HEADS, DIFF_VD, ATT_TK),
            tiles(FOX_HEADS, LANES, ATT_TQ),
            tiles(FOX_HEADS, FOX_DH, ATT_TK),
        ],
        out_shape=[
            jax.ShapeDtypeStruct((B, DIFF_HEADS, S, LANES), bf16),
            jax.ShapeDtypeStruct((B, FOX_HEADS, S, LANES), bf16),
            jax.ShapeDtypeStruct((B, DIFF_HEADS, S // ATT_TQ, 2 * DIFF_DH, ATT_TQ), bf16),
            jax.ShapeDtypeStruct((B, DIFF_HEADS, S // ATT_TK, DIFF_VD, ATT_TK), bf16),
            jax.ShapeDtypeStruct((B, FOX_HEADS, S // ATT_TQ, LANES, ATT_TQ), bf16),
            jax.ShapeDtypeStruct((B, FOX_HEADS, S // ATT_TK, FOX_DH, ATT_TK), bf16),
        ],
        scratch_shapes=[pltpu.VMEM((8, LANES), jnp.float32), pltpu.VMEM((TR_FG, LANES), jnp.float32)],
        compiler_params=pltpu.CompilerParams(
            dimension_semantics=("arbitrary", "arbitrary"), vmem_limit_bytes=VMEM_LIMIT_BYTES),
        name="proj",
    )(x, g1, wnat, wtr, bfn, bft, cn, s1n, s2n, ct, st, tri, en, pt, onesn)


def _sweep_tables(S, chunk):
    nq, ratio = S // ATT_TQ, ATT_TK // ATT_TQ
    diag = [(i, i // ratio, i % ratio) for i in range(nq)]
    full = [(i, j, 0) for i in range(nq) for j in range(i // ratio)]
    assert len(full) % (2 * ATT_GROUP) == 0 and len(diag) % (2 * ATT_GROUP) == 0
    kk = np.arange(ATT_TK)[:, None] // chunk
    masks = []
    for r in range(ratio):
        qq = (r * ATT_TQ + np.arange(ATT_TQ))[None, :] // chunk
        masks.append(np.where(kk <= qq, 0.0, NEG).astype(np.float32))
    as_tab = lambda pairs: np.array(pairs, np.int32).T.copy()
    return as_tab(full), as_tab(diag), np.stack(masks)


def _init_sweep_state(m_s, acc_s):
    m_s[...] = jnp.full_like(m_s, NEG)
    acc_s[...] = jnp.zeros_like(acc_s)


def _with_ones_rows(v_t):
    return jnp.concatenate([v_t, jnp.ones((ATT_ONES_ROWS, v_t.shape[1]), v_t.dtype)], axis=0)


def _flat_sweep(tab_ref, mask_ref, n_heads, scores, values, m_s, acc_s, sbuf, xbuf, pbuf, abuf):
    n_groups = tab_ref.shape[1] // ATT_GROUP
    assert n_groups >= 2 and n_groups % 2 == 0

    def slot(par, a):
        return par * ATT_GROUP + a

    def stage_scores(u, par):
        for a in range(ATT_GROUP):
            t = u * ATT_GROUP + a
            it, jt = tab_ref[0, t], tab_ref[1, t]
            for h in range(n_heads):
                s = scores(h, it, jt)
                if mask_ref is not None:
                    s = s + mask_ref[tab_ref[2, t]]
                sbuf[slot(par, a), h] = s
                xbuf[slot(par, a), h] = jnp.max(s, axis=0, keepdims=True)

    def stage_softmax(u, par):
        for a in range(ATT_GROUP):
            it = tab_ref[0, u * ATT_GROUP + a]
            for h in range(n_heads):
                m_old = m_s[h, it]
                m_new = jnp.maximum(m_old, xbuf[slot(par, a), h])
                alpha = jnp.exp2(m_old - m_new)
                p = jnp.exp2(sbuf[slot(par, a), h] - m_new)
                m_s[h, it] = m_new
                pbuf[slot(par, a), h] = p.astype(jnp.bfloat16)
                abuf[slot(par, a), h] = alpha

    def stage_values(u, par):
        for a in range(ATT_GROUP):
            t = u * ATT_GROUP + a
            it, jt = tab_ref[0, t], tab_ref[1, t]
            for h in range(n_heads):
                pv = jnp.dot(_with_ones_rows(values(h, jt)), pbuf[slot(par, a), h],
                             preferred_element_type=jnp.float32)
                acc_s[h, it] = abuf[slot(par, a), h] * acc_s[h, it] + pv

    def step(u, par, with_values=True):
        stage_scores(u, par)
        if with_values:
            stage_values(u - 2, par)
        stage_softmax(u - 1, 1 - par)

    stage_scores(0, 0)
    step(1, 1, with_values=False)

    def body(v, c):
        step(2 * v, 0)
        step(2 * v + 1, 1)
        return c

    lax.fori_loop(1, n_groups // 2, body, 0)
    stage_values(n_groups - 2, 0)
    stage_softmax(n_groups - 1, 1)
    stage_values(n_groups - 1, 1)


def _sweep_scratch(n_heads, nq, dv):
    f32 = jnp.float32
    slots = 2 * ATT_GROUP
    return [
        pltpu.VMEM((n_heads, nq, 1, ATT_TQ), f32),
        pltpu.VMEM((n_heads, nq, dv + ATT_ONES_ROWS, ATT_TQ), f32),
        pltpu.VMEM((slots, n_heads, ATT_TK, ATT_TQ), f32),
        pltpu.VMEM((slots, n_heads, 1, ATT_TQ), f32),
        pltpu.VMEM((slots, n_heads, ATT_TK, ATT_TQ), jnp.bfloat16),
        pltpu.VMEM((slots, n_heads, 1, ATT_TQ), f32),
    ]


def _key_slice(jt):
    return pl.ds(pl.multiple_of(jt * ATT_TK, ATT_TK), ATT_TK)


def _query_slice(it):
    return pl.ds(pl.multiple_of(it * ATT_TQ, ATT_TQ), ATT_TQ)


def _dattn_kernel(full_ref, diag_ref, q_ref, k_ref, v_ref, mask_ref, lq1_ref, lk1_ref, lq2_ref, lk2_ref, g_ref,
                  o_ref, m_s, acc_s, sbuf, xbuf, pbuf, abuf):
    nq = m_s.shape[1]
    row = lax.broadcasted_iota(jnp.int32, (2 * DIFF_DH, ATT_TQ), 0)

    def scores(h, it, jt):
        q = q_ref[0, 0, it]
        keep = (row < DIFF_DH) if h == 0 else (row >= DIFF_DH)
        rhs = jnp.where(keep, q, jnp.zeros_like(q))
        return jnp.dot(k_ref[0, 0, _key_slice(jt), :], rhs, preferred_element_type=jnp.float32)

    def values(h, jt):
        return v_ref[0, 0, jt]

    _init_sweep_state(m_s, acc_s)
    _flat_sweep(full_ref, None, 2, scores, values, m_s, acc_s, sbuf, xbuf, pbuf, abuf)
    _flat_sweep(diag_ref, mask_ref, 2, scores, values, m_s, acc_s, sbuf, xbuf, pbuf, abuf)

    lam = (jnp.exp(jnp.sum(lq1_ref[...] * lk1_ref[...], axis=-1, keepdims=True))
           - jnp.exp(jnp.sum(lq2_ref[...] * lk2_ref[...], axis=-1, keepdims=True)) + LAM_INIT)
    gain = g_ref[...] * (1.0 - LAM_INIT)

    def finalize(it, c):
        a1, a2 = acc_s[0, it], acc_s[1, it]
        o = (a1[:DIFF_VD] / a1[DIFF_VD:DIFF_VD + 1]
             - lam * (a2[:DIFF_VD] / a2[DIFF_VD:DIFF_VD + 1]))
        ms = jnp.mean(o * o, axis=0, keepdims=True)
        o = o * lax.rsqrt(ms + EPS) * gain
        o_ref[0, _query_slice(it), :] = o.T.astype(o_ref.dtype)
        return c

    lax.fori_loop(0, nq, finalize, 0)


def _dattn_call(full, diag, masks, qd, kd, vd, lq1, lk1, lq2, lk2, gcol):
    B, _, nq = qd.shape[:3]
    S = nq * ATT_TQ
    vec = lambda: pl.BlockSpec((1, DIFF_DH), lambda b, h, *_: (0, 0))
    return pl.pallas_call(
        _dattn_kernel,
        grid_spec=pltpu.PrefetchScalarGridSpec(
            num_scalar_prefetch=2,
            grid=(B, DIFF_HEADS),
            in_specs=[
                pl.BlockSpec((1, 1) + qd.shape[2:], lambda b, h, *_: (b, h, 0, 0, 0)),
                pl.BlockSpec((1, 1, S, LANES), lambda b, h, *_: (b, h, 0, 0)),
                pl.BlockSpec((1, 1) + vd.shape[2:], lambda b, h, *_: (b, h, 0, 0, 0)),
                pl.BlockSpec(masks.shape, lambda b, h, *_: (0, 0, 0)),
                vec(), vec(), vec(), vec(),
                pl.BlockSpec((DIFF_VD, ATT_TQ), lambda b, h, *_: (0, 0)),
            ],
            out_specs=pl.BlockSpec((1, S, DIFF_VD), lambda b, h, *_: (b, 0, h)),
            scratch_shapes=_sweep_scratch(2, nq, DIFF_VD),
        ),
        out_shape=jax.ShapeDtypeStruct((B, S, DIFF_HEADS * DIFF_VD), jnp.bfloat16),
        compiler_params=pltpu.CompilerParams(
            dimension_semantics=("arbitrary", "arbitrary"), vmem_limit_bytes=VMEM_LIMIT_BYTES),
        name="dattn",
    )(full, diag, qd, kd, vd, masks, lq1, lk1, lq2, lk2, gcol)


def _fattn_kernel(full_ref, diag_ref, q_ref, k_ref, v_ref, mask_ref, o_ref, m_s, acc_s, sbuf, xbuf, pbuf, abuf):
    nq = m_s.shape[1]

    def scores(h, it, jt):
        q = q_ref[0, h, it]
        return jnp.dot(k_ref[0, h, _key_slice(jt), :], q, preferred_element_type=jnp.float32)

    def values(h, jt):
        return v_ref[0, h, jt]

    _init_sweep_state(m_s, acc_s)
    _flat_sweep(full_ref, None, 2, scores, values, m_s, acc_s, sbuf, xbuf, pbuf, abuf)
    _flat_sweep(diag_ref, mask_ref, 2, scores, values, m_s, acc_s, sbuf, xbuf, pbuf, abuf)

    def finalize(it, c):
        heads = [acc_s[h, it] for h in range(2)]
        o = jnp.concatenate([a[:FOX_DH] / a[FOX_DH:FOX_DH + 1] for a in heads], axis=0)
        o_ref[0, _query_slice(it), :] = o.T.astype(o_ref.dtype)
        return c

    lax.fori_loop(0, nq, finalize, 0)


def _fattn_call(full, diag, masks, qf, kf, vf):
    B, _, nq = qf.shape[:3]
    S = nq * ATT_TQ
    return pl.pallas_call(
        _fattn_kernel,
        grid_spec=pltpu.PrefetchScalarGridSpec(
            num_scalar_prefetch=2,
            grid=(B, FOX_HEADS // 2),
            in_specs=[
                pl.BlockSpec((1, 2) + qf.shape[2:], lambda b, p, *_: (b, p, 0, 0, 0)),
                pl.BlockSpec((1, 2, S, LANES), lambda b, p, *_: (b, p, 0, 0)),
                pl.BlockSpec((1, 2) + vf.shape[2:], lambda b, p, *_: (b, p, 0, 0, 0)),
                pl.BlockSpec(masks.shape, lambda b, p, *_: (0, 0, 0)),
            ],
            out_specs=pl.BlockSpec((1, S, 2 * FOX_DH), lambda b, p, *_: (b, 0, p)),
            scratch_shapes=_sweep_scratch(2, nq, FOX_DH),
        ),
        out_shape=jax.ShapeDtypeStruct((B, S, FOX_HEADS * FOX_DH), jnp.bfloat16),
        compiler_params=pltpu.CompilerParams(
            dimension_semantics=("arbitrary", "arbitrary"), vmem_limit_bytes=VMEM_LIMIT_BYTES),
        name="fattn",
    )(full, diag, qf, kf, vf, masks)


def _ffn_kernel(x_ref, md_ref, mf_ref, wod_ref, wof_ref, g2_ref, wg_ref, wu_ref, wd_ref, gf_ref, o_ref):
    x1 = (x_ref[...]
          + jnp.dot(md_ref[...], wod_ref[...], preferred_element_type=jnp.float32)
          + jnp.dot(mf_ref[...], wof_ref[...], preferred_element_type=jnp.float32))
    ms = jnp.mean(x1 * x1, axis=-1, keepdims=True)
    h2 = (x1 * lax.rsqrt(ms + EPS) * g2_ref[...]).astype(jnp.bfloat16)
    g = jnp.dot(h2, wg_ref[...], preferred_element_type=jnp.float32)
    u = jnp.dot(h2, wu_ref[...], preferred_element_type=jnp.float32)
    act = (g * jax.nn.sigmoid(g) * u).astype(jnp.bfloat16)
    y = x1 + jnp.dot(act, wd_ref[...], preferred_element_type=jnp.float32)
    ms = jnp.mean(y * y, axis=-1, keepdims=True)
    o_ref[...] = y * lax.rsqrt(ms + EPS) * gf_ref[...]


def _ffn_call(x2, md, mf, wod, wof, g2, wg, wu, wd, gf):
    n, D = x2.shape
    tm = FFN_TM
    half = md.shape[1]
    resident = lambda shape: pl.BlockSpec(shape, lambda i: (0, 0), pipeline_mode=pl.Buffered(1))
    return pl.pallas_call(
        _ffn_kernel,
        grid=(n // tm,),
        in_specs=[
            pl.BlockSpec((tm, D), lambda i: (i, 0)),
            pl.BlockSpec((tm, half), lambda i: (i, 0)),
            pl.BlockSpec((tm, half), lambda i: (i, 0)),
            resident((half, D)),
            resident((half, D)),
            resident((1, D)),
            resident((D, D_FF)),
            resident((D, D_FF)),
            resident((D_FF, D)),
            resident((1, D)),
        ],
        out_specs=pl.BlockSpec((tm, D), lambda i: (i, 0)),
        out_shape=jax.ShapeDtypeStruct((n, D), jnp.float32),
        compiler_params=pltpu.CompilerParams(
            dimension_semantics=("arbitrary",), vmem_limit_bytes=VMEM_LIMIT_BYTES),
        name="ffn",
    )(x2, md, mf, wod, wof, g2, wg, wu, wd, gf)


def _placement_tables():
    en = np.zeros((3 * LANES, NAT_FK), np.float32)
    pt = np.zeros((TR_FQ, 4 * TR_FG), np.float32)
    onesn = np.zeros((1, NAT_FK), np.float32)
    for h in range(FOX_HEADS):
        for p in range(3):
            en[p * LANES + h, h * LANES + AUG_BIAS + p] = 1.0
            onesn[0, h * LANES + AUG_ONES + p] = 1.0
            pt[h * LANES + AUG_ONES + p, p * TR_FG + h] = 1.0
            pt[h * LANES + AUG_BIAS + p, 3 * TR_FG] = 1.0
    tri = np.tril(np.ones((PROJ_TM, PROJ_TM), np.float32))
    return en, pt, onesn, tri


def _rope_tables(S):
    f32 = jnp.float32
    pos = jnp.arange(S, dtype=jnp.int32)
    inv_freq = ROPE_THETA ** (-jnp.arange(0, ROPE_DIM, 2, dtype=f32) / ROPE_DIM)
    ang = pos.astype(f32)[:, None] * inv_freq[None, :]
    cos, sin = jnp.cos(ang), jnp.sin(ang)
    lane = np.arange(LANES) % DIFF_DH
    sel = lane % ROPE_HALF
    lo = jnp.asarray(lane < ROPE_HALF)[None, :]
    hi = jnp.asarray((lane >= ROPE_HALF) & (lane < ROPE_DIM))[None, :]
    cos_l, sin_l = cos[:, sel], sin[:, sel]
    cn = jnp.where(lo | hi, cos_l, 1.0)
    s1n = jnp.where(lo, -sin_l, 0.0)
    s2n = jnp.where(hi, sin_l, 0.0)
    return cn, s1n, s2n, cos.T, sin.T


def kernel(x, norm1_g, w_in, b_f, lam_q1, lam_k1, lam_q2, lam_k2, subln_g, w_out, norm2_g, w_gate, w_up,
           w_down, normf_g):
    f32, bf16 = jnp.float32, jnp.bfloat16
    B, S, D = x.shape
    en, pt, onesn, tri = _placement_tables()
    cn, s1n, s2n, ct, st = _rope_tables(S)

    w = w_in[0]
    o_dq, o_dk, o_dv, o_fq, o_fk, o_fv, o_fg = 0, 512, 1024, 1536, 2048, 2560, 3072
    pad_heads = lambda m: jnp.pad(m.reshape(D, FOX_HEADS, FOX_DH), ((0, 0), (0, 0), (0, LANES - FOX_DH))
                                  ).reshape(D, FOX_HEADS * LANES)
    wnat = jnp.concatenate(
        [w[:, o_dk:o_dv], pad_heads(w[:, o_fk:o_fv]), jnp.pad(w[:, o_fg:], ((0, 0), (0, LANES - FOX_HEADS)))],
        axis=1).astype(bf16)
    wtr = jnp.concatenate(
        [w[:, o_dq:o_dk], w[:, o_dv:o_fq], pad_heads(w[:, o_fq:o_fk]), w[:, o_fv:o_fg],
         jnp.pad(w[:, o_fg:], ((0, 0), (0, TR_FG - FOX_HEADS)))], axis=1).T.astype(bf16)
    bfn = jnp.pad(b_f[0].astype(f32), (0, LANES - FOX_HEADS))[None, :]
    bft = jnp.broadcast_to(jnp.pad(b_f[0].astype(f32), (0, TR_FG - FOX_HEADS))[:, None], (TR_FG, PROJ_TM))

    kd, kf, qd, vd, qf, vf = _proj_call(
        x, norm1_g[0][None, :], wnat, wtr, bfn, bft, cn, s1n, s2n, ct, st,
        jnp.asarray(tri, bf16), jnp.asarray(en, bf16), jnp.asarray(pt, bf16), jnp.asarray(onesn, f32))

    gcol = jnp.broadcast_to(subln_g[0].astype(f32)[:, None], (DIFF_VD, ATT_TQ))
    row = lambda v: v[0].astype(f32)[None, :]
    full, diag, masks_d = _sweep_tables(S, CHUNK)
    _, _, masks_f = _sweep_tables(S, 1)
    full, diag = jnp.asarray(full), jnp.asarray(diag)
    mixd = _dattn_call(full, diag, jnp.asarray(masks_d), qd, kd, vd,
                       row(lam_q1), row(lam_k1), row(lam_q2), row(lam_k2), gcol)
    mixf = _fattn_call(full, diag, jnp.asarray(masks_f), qf, kf, vf)

    wo = w_out[0].astype(bf16)
    half = DIFF_HEADS * DIFF_VD
    out = _ffn_call(
        x.reshape(B * S, D), mixd.reshape(B * S, half), mixf.reshape(B * S, half),
        wo[:half], wo[half:], norm2_g[0][None, :], w_gate[0].astype(bf16), w_up[0].astype(bf16),
        w_down[0].astype(bf16), normf_g[None, :])
    return out.reshape(B, S, D)
```

```python
import math

import jax
import jax.numpy as jnp
import numpy as np
from jax import lax
from jax.experimental import pallas as pl
from jax.experimental.pallas import tpu as pltpu

D_MODEL = 1024
CHUNK = 64
DIFF_HEADS = 4
DIFF_DH = 64
DIFF_VD = 128
FOX_HEADS = 8
FOX_DH = 64
ROPE_THETA = 500000.0
ROPE_DIM = 16
ROPE_HALF = 8
D_FF = 2816
EPS = 1e-5
NEG = -1e30
LAM_INIT = 0.8 - 0.6 * math.exp(-0.3 * 0)
LOG2E = 1.4426950408889634

LANES = 128
MXU_N = 256
VMEM_LIMIT_BYTES = 56 * 1024 * 1024

PROJ_TM = 512
ATT_TQ = MXU_N
ATT_TK = 512
ATT_GROUP = 2
ATT_UNROLL = 2
ATT_ONES_ROWS = 16
FFN_TM = 512

NAT_DK = DIFF_HEADS * 2 * DIFF_DH
NAT_FK = FOX_HEADS * FOX_DH
NAT_W = NAT_DK + NAT_FK
TR_DQ = DIFF_HEADS * 2 * DIFF_DH
TR_DV = DIFF_HEADS * DIFF_VD
TR_FQ = FOX_HEADS * FOX_DH
TR_FV = FOX_HEADS * FOX_DH
TR_FG = 16
TR_W = TR_DQ + TR_DV + TR_FQ + TR_FV + TR_FG
AUG_ONES = FOX_DH
AUG_BIAS = FOX_DH + 3
AUG_ROWS = 16
KF_W = FOX_HEADS * LANES


def _nt_dot(a, b):
    return lax.dot_general(a, b, (((1,), (1,)), ((), ())), preferred_element_type=jnp.float32)


def _split3(x):
    hi = x.astype(jnp.bfloat16)
    r = x - hi.astype(jnp.float32)
    mid = r.astype(jnp.bfloat16)
    lo = (r - mid.astype(jnp.float32)).astype(jnp.bfloat16)
    return hi, mid, lo


def _log_sigmoid(x):
    return jnp.minimum(x, 0.0) - jnp.log1p(jnp.exp(-jnp.abs(x)))


def _store_tiles(ref, x, rows, cols):
    for hd in range(x.shape[0] // rows):
        for c in range(x.shape[1] // cols):
            ref[0, hd, c] = x[hd * rows:(hd + 1) * rows, c * cols:(c + 1) * cols]


def _proj_kernel(x_ref, g_ref, wnat_ref, wtr_ref, bft_ref, cn_ref, s1n_ref, s2n_ref, ct_ref, st_ref,
                 tri_ref, en_ref, pt_ref, onesn_ref,
                 kd_ref, kf_ref, qd_ref, vd_ref, qf_ref, vf_ref, carry_t):
    @pl.when(pl.program_id(1) == 0)
    def _():
        carry_t[...] = jnp.zeros_like(carry_t)

    f32, bf16 = jnp.float32, jnp.bfloat16
    tm = x_ref.shape[1]
    x = x_ref[0]
    ms = jnp.mean(x * x, axis=-1, keepdims=True)
    h = (x * lax.rsqrt(ms + EPS) * g_ref[...]).astype(bf16)

    nat = jnp.dot(h, wnat_ref[...], preferred_element_type=f32)
    tr = _nt_dot(wtr_ref[...], h)

    cn, s1n, s2n = cn_ref[...], s1n_ref[...], s2n_ref[...]
    for hd in range(DIFF_HEADS):
        t = nat[:, hd * LANES:(hd + 1) * LANES]
        rot = t * cn + pltpu.roll(t, LANES - ROPE_HALF, 1) * s1n + pltpu.roll(t, ROPE_HALF, 1) * s2n
        kd_ref[0, hd] = rot.astype(bf16)

    ct, st = ct_ref[...], st_ref[...]
    pieces = []
    for comp in range(DIFF_HEADS * 2):
        base = comp * DIFF_DH
        t1 = tr[base:base + ROPE_HALF]
        t2 = tr[base + ROPE_HALF:base + ROPE_DIM]
        pieces += [t1 * ct - t2 * st, t2 * ct + t1 * st, tr[base + ROPE_DIM:base + DIFF_DH]]
    qd = jnp.concatenate(pieces, axis=0) * (DIFF_DH ** -0.5 * LOG2E)
    _store_tiles(qd_ref, qd.astype(bf16), 2 * DIFF_DH, ATT_TQ)
    _store_tiles(vd_ref, tr[TR_DQ:TR_DQ + TR_DV].astype(bf16), DIFF_VD, ATT_TK)

    o_fq = TR_DQ + TR_DV
    o_fv = o_fq + TR_FQ
    o_fg = o_fv + TR_FV
    lf_t = _log_sigmoid(tr[o_fg:o_fg + TR_FG] + bft_ref[...])
    c3t = _nt_dot(jnp.concatenate(_split3(lf_t), axis=0), tri_ref[...])
    cum_t = (c3t[2 * TR_FG:] + c3t[TR_FG:2 * TR_FG]) + c3t[:TR_FG] + carry_t[:, 0:1]
    carry_t[...] = jnp.broadcast_to(cum_t[:, tm - 1:tm], carry_t.shape)

    qb = jnp.concatenate(_split3(cum_t * LOG2E) + (jnp.ones((TR_FG, tm), bf16),), axis=0)
    aug_q = jnp.dot(pt_ref[...], qb, preferred_element_type=f32)
    fq = tr[o_fq:o_fq + TR_FQ] * (FOX_DH ** -0.5 * LOG2E)
    pad_rows = jnp.zeros((LANES - FOX_DH - AUG_ROWS, tm), f32)
    for hd in range(FOX_HEADS):
        blk = jnp.concatenate([fq[hd * FOX_DH:(hd + 1) * FOX_DH], aug_q[hd * AUG_ROWS:(hd + 1) * AUG_ROWS],
                               pad_rows], axis=0).astype(bf16)
        for c in range(tm // ATT_TQ):
            qf_ref[0, hd, c] = blk[:, c * ATT_TQ:(c + 1) * ATT_TQ]
    _store_tiles(vf_ref, tr[o_fv:o_fv + TR_FV].astype(bf16), FOX_DH, ATT_TK)

    parts = [part.astype(f32)[:FOX_HEADS] for part in _split3(cum_t * (-LOG2E))]
    parts.append(jnp.zeros((LANES - 3 * FOX_HEADS, tm), f32))
    kb = jnp.concatenate(parts, axis=0).T.astype(bf16)
    aug_k = jnp.dot(kb, en_ref[...], preferred_element_type=f32) + onesn_ref[...]
    lane = lax.broadcasted_iota(jnp.int32, (tm, LANES), 1)
    for hd in range(FOX_HEADS):
        src = nat[:, NAT_DK + (hd // 2) * LANES:NAT_DK + (hd // 2 + 1) * LANES]
        if hd % 2:
            src = pltpu.roll(src, FOX_DH, 1)
        kf_ref[0, hd] = jnp.where(lane < FOX_DH, src, aug_k[:, hd * LANES:(hd + 1) * LANES]).astype(bf16)


def _proj_call(x, g1, wnat, wtr, bft, cn, s1n, s2n, ct, st, tri, en, pt, onesn):
    B, S, D = x.shape
    tm = PROJ_TM
    const = lambda shape: pl.BlockSpec(shape, lambda b, i: (0,) * len(shape))
    tiles = lambda heads, rows, cols: pl.BlockSpec((1, heads, tm // cols, rows, cols), lambda b, i: (b, 0, i, 0, 0))
    bf16 = jnp.bfloat16
    return pl.pallas_call(
        _proj_kernel,
        grid=(B, S // tm),
        in_specs=[
            pl.BlockSpec((1, tm, D), lambda b, i: (b, i, 0)),
            const((1, D)),
            const((D, NAT_W)),
            const((TR_W, D)),
            const((TR_FG, tm)),
            pl.BlockSpec((tm, LANES), lambda b, i: (i, 0)),
            pl.BlockSpec((tm, LANES), lambda b, i: (i, 0)),
            pl.BlockSpec((tm, LANES), lambda b, i: (i, 0)),
            pl.BlockSpec((ROPE_HALF, tm), lambda b, i: (0, i)),
            pl.BlockSpec((ROPE_HALF, tm), lambda b, i: (0, i)),
            const((tm, tm)),
            const((LANES, KF_W)),
            const((FOX_HEADS * AUG_ROWS, 4 * TR_FG)),
            const((1, KF_W)),
        ],
        out_specs=[
            pl.BlockSpec((1, DIFF_HEADS, tm, LANES), lambda b, i: (b, 0, i, 0)),
            pl.BlockSpec((1, FOX_HEADS, tm, LANES), lambda b, i: (b, 0, i, 0)),
            tiles(DIFF_HEADS, 2 * DIFF_DH, ATT_TQ),
            tiles(DIFF_HEADS, DIFF_VD, ATT_TK),
            tiles(FOX_HEADS, LANES, ATT_TQ),
            tiles(FOX_HEADS, FOX_DH, ATT_TK),
        ],
        out_shape=[
            jax.ShapeDtypeStruct((B, DIFF_HEADS, S, LANES), bf16),
            jax.ShapeDtypeStruct((B, FOX_HEADS, S, LANES), bf16),
            jax.ShapeDtypeStruct((B, DIFF_HEADS, S // ATT_TQ, 2 * DIFF_DH, ATT_TQ), bf16),
            jax.ShapeDtypeStruct((B, DIFF_HEADS, S // ATT_TK, DIFF_VD, ATT_TK), bf16),
            jax.ShapeDtypeStruct((B, FOX_HEADS, S // ATT_TQ, LANES, ATT_TQ), bf16),
            jax.ShapeDtypeStruct((B, FOX_HEADS, S // ATT_TK, FOX_DH, ATT_TK), bf16),
        ],
        scratch_shapes=[pltpu.VMEM((TR_FG, LANES), jnp.float32)],
        compiler_params=pltpu.CompilerParams(
            dimension_semantics=("arbitrary", "arbitrary"), vmem_limit_bytes=VMEM_LIMIT_BYTES),
        name="proj",
    )(x, g1, wnat, wtr, bft, cn, s1n, s2n, ct, st, tri, en, pt, onesn)


def _sweep_tables(S, chunk):
    nq, ratio = S // ATT_TQ, ATT_TK // ATT_TQ
    diag = [(i, i // ratio, i % ratio) for i in range(nq)]
    full = [(i, j, 0) for i in range(nq) for j in range(i // ratio)]
    assert len(full) % (2 * ATT_GROUP) == 0 and len(diag) % (2 * ATT_GROUP) == 0
    kk = np.arange(ATT_TK)[:, None] // chunk
    masks = []
    for r in range(ratio):
        qq = (r * ATT_TQ + np.arange(ATT_TQ))[None, :] // chunk
        masks.append(np.where(kk <= qq, 0.0, NEG).astype(np.float32))
    as_tab = lambda pairs: np.array(pairs, np.int32).T.copy()
    return as_tab(full), as_tab(diag), np.stack(masks)


def _init_sweep_state(m_s, acc_s):
    m_s[...] = jnp.full_like(m_s, NEG)
    acc_s[...] = jnp.zeros_like(acc_s)


def _with_ones_rows(v_t):
    return jnp.concatenate([v_t, jnp.ones((ATT_ONES_ROWS, v_t.shape[1]), v_t.dtype)], axis=0)


def _flat_sweep(tab_ref, mask_ref, n_heads, scores, values, m_s, acc_s, sbuf, xbuf, pbuf, abuf):
    n_groups = tab_ref.shape[1] // ATT_GROUP
    assert n_groups >= 2 and n_groups % 2 == 0

    def slot(par, a):
        return par * ATT_GROUP + a

    def stage_scores(u, par):
        for a in range(ATT_GROUP):
            t = u * ATT_GROUP + a
            it, jt = tab_ref[0, t], tab_ref[1, t]
            for h in range(n_heads):
                s = scores(h, it, jt)
                if mask_ref is not None:
                    s = s + mask_ref[tab_ref[2, t]]
                sbuf[slot(par, a), h] = s
                xbuf[slot(par, a), h] = jnp.max(s, axis=0, keepdims=True)

    def stage_softmax(u, par):
        for a in range(ATT_GROUP):
            it = tab_ref[0, u * ATT_GROUP + a]
            for h in range(n_heads):
                m_old = m_s[h, it]
                m_new = jnp.maximum(m_old, xbuf[slot(par, a), h])
                alpha = jnp.exp2(m_old - m_new)
                p = jnp.exp2(sbuf[slot(par, a), h] - m_new)
                m_s[h, it] = m_new
                pbuf[slot(par, a), h] = p.astype(jnp.bfloat16)
                abuf[slot(par, a), h] = alpha

    def stage_values(u, par):
        for a in range(ATT_GROUP):
            t = u * ATT_GROUP + a
            it, jt = tab_ref[0, t], tab_ref[1, t]
            for h in range(n_heads):
                pv = jnp.dot(_with_ones_rows(values(h, jt)), pbuf[slot(par, a), h],
                             preferred_element_type=jnp.float32)
                acc_s[h, it] = abuf[slot(par, a), h] * acc_s[h, it] + pv

    def step(u, par, with_values=True):
        stage_scores(u, par)
        if with_values:
            stage_values(u - 2, par)
        stage_softmax(u - 1, 1 - par)

    stage_scores(0, 0)
    step(1, 1, with_values=False)
    first = 2 + (n_groups - 2) % ATT_UNROLL
    for u in range(2, first):
        step(u, u % 2)

    def body(v, c):
        for k in range(ATT_UNROLL):
            step(first + ATT_UNROLL * v + k, (first + k) % 2)
        return c

    lax.fori_loop(0, (n_groups - first) // ATT_UNROLL, body, 0)
    stage_values(n_groups - 2, 0)
    stage_softmax(n_groups - 1, 1)
    stage_values(n_groups - 1, 1)


def _sweep_scratch(n_heads, nq, dv):
    f32 = jnp.float32
    slots = 2 * ATT_GROUP
    return [
        pltpu.VMEM((n_heads, nq, 1, ATT_TQ), f32),
        pltpu.VMEM((n_heads, nq, dv + ATT_ONES_ROWS, ATT_TQ), f32),
        pltpu.VMEM((slots, n_heads, ATT_TK, ATT_TQ), f32),
        pltpu.VMEM((slots, n_heads, 1, ATT_TQ), f32),
        pltpu.VMEM((slots, n_heads, ATT_TK, ATT_TQ), jnp.bfloat16),
        pltpu.VMEM((slots, n_heads, 1, ATT_TQ), f32),
    ]


def _key_slice(jt):
    return pl.ds(pl.multiple_of(jt * ATT_TK, ATT_TK), ATT_TK)


def _query_slice(it):
    return pl.ds(pl.multiple_of(it * ATT_TQ, ATT_TQ), ATT_TQ)


def _dattn_kernel(full_ref, diag_ref, q_ref, k_ref, v_ref, mask_ref, lq1_ref, lk1_ref, lq2_ref, lk2_ref, g_ref,
                  o_ref, m_s, acc_s, sbuf, xbuf, pbuf, abuf):
    nq = m_s.shape[1]
    row = lax.broadcasted_iota(jnp.int32, (2 * DIFF_DH, ATT_TQ), 0)

    def scores(h, it, jt):
        q = q_ref[0, 0, it]
        keep = (row < DIFF_DH) if h == 0 else (row >= DIFF_DH)
        rhs = jnp.where(keep, q, jnp.zeros_like(q))
        return jnp.dot(k_ref[0, 0, _key_slice(jt), :], rhs, preferred_element_type=jnp.float32)

    def values(h, jt):
        return v_ref[0, 0, jt]

    _init_sweep_state(m_s, acc_s)
    _flat_sweep(full_ref, None, 2, scores, values, m_s, acc_s, sbuf, xbuf, pbuf, abuf)
    _flat_sweep(diag_ref, mask_ref, 2, scores, values, m_s, acc_s, sbuf, xbuf, pbuf, abuf)

    lam = (jnp.exp(jnp.sum(lq1_ref[...] * lk1_ref[...], axis=-1, keepdims=True))
           - jnp.exp(jnp.sum(lq2_ref[...] * lk2_ref[...], axis=-1, keepdims=True)) + LAM_INIT)
    gain = g_ref[...] * (1.0 - LAM_INIT)

    def finalize(it, c):
        a1, a2 = acc_s[0, it], acc_s[1, it]
        o = (a1[:DIFF_VD] / a1[DIFF_VD:DIFF_VD + 1]
             - lam * (a2[:DIFF_VD] / a2[DIFF_VD:DIFF_VD + 1]))
        ms = jnp.mean(o * o, axis=0, keepdims=True)
        o = o * lax.rsqrt(ms + EPS) * gain
        o_ref[0, _query_slice(it), :] = o.T.astype(o_ref.dtype)
        return c

    lax.fori_loop(0, nq, finalize, 0)


def _dattn_call(full, diag, masks, qd, kd, vd, lq1, lk1, lq2, lk2, gcol):
    B, _, nq = qd.shape[:3]
    S = nq * ATT_TQ
    vec = lambda: pl.BlockSpec((1, DIFF_DH), lambda b, h, *_: (0, 0))
    return pl.pallas_call(
        _dattn_kernel,
        grid_spec=pltpu.PrefetchScalarGridSpec(
            num_scalar_prefetch=2,
            grid=(B, DIFF_HEADS),
            in_specs=[
                pl.BlockSpec((1, 1) + qd.shape[2:], lambda b, h, *_: (b, h, 0, 0, 0)),
                pl.BlockSpec((1, 1, S, LANES), lambda b, h, *_: (b, h, 0, 0)),
                pl.BlockSpec((1, 1) + vd.shape[2:], lambda b, h, *_: (b, h, 0, 0, 0)),
                pl.BlockSpec(masks.shape, lambda b, h, *_: (0, 0, 0)),
                vec(), vec(), vec(), vec(),
                pl.BlockSpec((DIFF_VD, ATT_TQ), lambda b, h, *_: (0, 0)),
            ],
            out_specs=pl.BlockSpec((1, S, DIFF_VD), lambda b, h, *_: (b, 0, h)),
            scratch_shapes=_sweep_scratch(2, nq, DIFF_VD),
        ),
        out_shape=jax.ShapeDtypeStruct((B, S, DIFF_HEADS * DIFF_VD), jnp.bfloat16),
        compiler_params=pltpu.CompilerParams(
            dimension_semantics=("arbitrary", "arbitrary"), vmem_limit_bytes=VMEM_LIMIT_BYTES),
        name="dattn",
    )(full, diag, qd, kd, vd, masks, lq1, lk1, lq2, lk2, gcol)


def _fattn_kernel(full_ref, diag_ref, q_ref, k_ref, v_ref, mask_ref, o_ref, m_s, acc_s, sbuf, xbuf, pbuf, abuf):
    nq = m_s.shape[1]

    def scores(h, it, jt):
        q = q_ref[0, h, it]
        return jnp.dot(k_ref[0, h, _key_slice(jt), :], q, preferred_element_type=jnp.float32)

    def values(h, jt):
        return v_ref[0, h, jt]

    _init_sweep_state(m_s, acc_s)
    _flat_sweep(full_ref, None, 2, scores, values, m_s, acc_s, sbuf, xbuf, pbuf, abuf)
    _flat_sweep(diag_ref, mask_ref, 2, scores, values, m_s, acc_s, sbuf, xbuf, pbuf, abuf)

    def finalize(it, c):
        heads = [acc_s[h, it] for h in range(2)]
        o = jnp.concatenate([a[:FOX_DH] / a[FOX_DH:FOX_DH + 1] for a in heads], axis=0)
        o_ref[0, _query_slice(it), :] = o.T.astype(o_ref.dtype)
        return c

    lax.fori_loop(0, nq, finalize, 0)


def _fattn_call(full, diag, masks, qf, kf, vf):
    B, _, nq = qf.shape[:3]
    S = nq * ATT_TQ
    return pl.pallas_call(
        _fattn_kernel,
        grid_spec=pltpu.PrefetchScalarGridSpec(
            num_scalar_prefetch=2,
            grid=(B, FOX_HEADS // 2),
            in_specs=[
                pl.BlockSpec((1, 2) + qf.shape[2:], lambda b, p, *_: (b, p, 0, 0, 0)),
                pl.BlockSpec((1, 2, S, LANES), lambda b, p, *_: (b, p, 0, 0)),
                pl.BlockSpec((1, 2) + vf.shape[2:], lambda b, p, *_: (b, p, 0, 0, 0)),
                pl.BlockSpec(masks.shape, lambda b, p, *_: (0, 0, 0)),
            ],
            out_specs=pl.BlockSpec((1, S, 2 * FOX_DH), lambda b, p, *_: (b, 0, p)),
            scratch_shapes=_sweep_scratch(2, nq, FOX_DH),
        ),
        out_shape=jax.ShapeDtypeStruct((B, S, FOX_HEADS * FOX_DH), jnp.bfloat16),
        compiler_params=pltpu.CompilerParams(
            dimension_semantics=("arbitrary", "arbitrary"), vmem_limit_bytes=VMEM_LIMIT_BYTES),
        name="fattn",
    )(full, diag, qf, kf, vf, masks)


def _ffn_kernel(x_ref, md_ref, mf_ref, wod_ref, wof_ref, g2_ref, wg_ref, wu_ref, wd_ref, gf_ref, o_ref):
    x1 = (x_ref[...]
          + jnp.dot(md_ref[...], wod_ref[...], preferred_element_type=jnp.float32)
          + jnp.dot(mf_ref[...], wof_ref[...], preferred_element_type=jnp.float32))
    ms = jnp.mean(x1 * x1, axis=-1, keepdims=True)
    h2 = (x1 * lax.rsqrt(ms + EPS) * g2_ref[...]).astype(jnp.bfloat16)
    g = jnp.dot(h2, wg_ref[...], preferred_element_type=jnp.float32)
    u = jnp.dot(h2, wu_ref[...], preferred_element_type=jnp.float32)
    act = (g * jax.nn.sigmoid(g) * u).astype(jnp.bfloat16)
    y = x1 + jnp.dot(act, wd_ref[...], preferred_element_type=jnp.float32)
    ms = jnp.mean(y * y, axis=-1, keepdims=True)
    o_ref[...] = y * lax.rsqrt(ms + EPS) * gf_ref[...]


def _ffn_call(x2, md, mf, wod, wof, g2, wg, wu, wd, gf):
    n, D = x2.shape
    tm = FFN_TM
    half = md.shape[1]
    resident = lambda shape: pl.BlockSpec(shape, lambda i: (0, 0), pipeline_mode=pl.Buffered(1))
    return pl.pallas_call(
        _ffn_kernel,
        grid=(n // tm,),
        in_specs=[
            pl.BlockSpec((tm, D), lambda i: (i, 0)),
            pl.BlockSpec((tm, half), lambda i: (i, 0)),
            pl.BlockSpec((tm, half), lambda i: (i, 0)),
            resident((half, D)),
            resident((half, D)),
            resident((1, D)),
            resident((D, D_FF)),
            resident((D, D_FF)),
            resident((D_FF, D)),
            resident((1, D)),
        ],
        out_specs=pl.BlockSpec((tm, D), lambda i: (i, 0)),
        out_shape=jax.ShapeDtypeStruct((n, D), jnp.float32),
        compiler_params=pltpu.CompilerParams(
            dimension_semantics=("arbitrary",), vmem_limit_bytes=VMEM_LIMIT_BYTES),
        name="ffn",
    )(x2, md, mf, wod, wof, g2, wg, wu, wd, gf)


def _placement_tables():
    en = np.zeros((LANES, KF_W), np.float32)
    pt = np.zeros((FOX_HEADS * AUG_ROWS, 4 * TR_FG), np.float32)
    onesn = np.zeros((1, KF_W), np.float32)
    for h in range(FOX_HEADS):
        for p in range(3):
            en[p * FOX_HEADS + h, h * LANES + AUG_BIAS + p] = 1.0
            onesn[0, h * LANES + AUG_ONES + p] = 1.0
            pt[h * AUG_ROWS + (AUG_ONES - FOX_DH) + p, p * TR_FG + h] = 1.0
            pt[h * AUG_ROWS + (AUG_BIAS - FOX_DH) + p, 3 * TR_FG] = 1.0
    tri = np.tril(np.ones((PROJ_TM, PROJ_TM), np.float32))
    return en, pt, onesn, tri


def _rope_tables(S):
    f32 = jnp.float32
    pos = jnp.arange(S, dtype=jnp.int32)
    inv_freq = ROPE_THETA ** (-jnp.arange(0, ROPE_DIM, 2, dtype=f32) / ROPE_DIM)
    ang = pos.astype(f32)[:, None] * inv_freq[None, :]
    cos, sin = jnp.cos(ang), jnp.sin(ang)
    lane = np.arange(LANES) % DIFF_DH
    sel = lane % ROPE_HALF
    lo = jnp.asarray(lane < ROPE_HALF)[None, :]
    hi = jnp.asarray((lane >= ROPE_HALF) & (lane < ROPE_DIM))[None, :]
    cos_l, sin_l = cos[:, sel], sin[:, sel]
    cn = jnp.where(lo | hi, cos_l, 1.0)
    s1n = jnp.where(lo, -sin_l, 0.0)
    s2n = jnp.where(hi, sin_l, 0.0)
    return cn, s1n, s2n, cos.T, sin.T


def kernel(x, norm1_g, w_in, b_f, lam_q1, lam_k1, lam_q2, lam_k2, subln_g, w_out, norm2_g, w_gate, w_up,
           w_down, normf_g):
    f32, bf16 = jnp.float32, jnp.bfloat16
    B, S, D = x.shape
    en, pt, onesn, tri = _placement_tables()
    cn, s1n, s2n, ct, st = _rope_tables(S)

    w = w_in[0]
    o_dq, o_dk, o_dv, o_fq, o_fk, o_fv, o_fg = 0, 512, 1024, 1536, 2048, 2560, 3072
    wnat = jnp.concatenate([w[:, o_dk:o_dv], w[:, o_fk:o_fv]], axis=1).astype(bf16)
    wtr = jnp.concatenate(
        [w[:, o_dq:o_dk], w[:, o_dv:o_fq], w[:, o_fq:o_fk], w[:, o_fv:o_fg],
         jnp.pad(w[:, o_fg:], ((0, 0), (0, TR_FG - FOX_HEADS)))], axis=1).T.astype(bf16)
    bft = jnp.broadcast_to(jnp.pad(b_f[0].astype(f32), (0, TR_FG - FOX_HEADS))[:, None], (TR_FG, PROJ_TM))

    kd, kf, qd, vd, qf, vf = _proj_call(
        x, norm1_g[0][None, :], wnat, wtr, bft, cn, s1n, s2n, ct, st,
        jnp.asarray(tri, bf16), jnp.asarray(en, bf16), jnp.asarray(pt, bf16), jnp.asarray(onesn, f32))

    gcol = jnp.broadcast_to(subln_g[0].astype(f32)[:, None], (DIFF_VD, ATT_TQ))
    row = lambda v: v[0].astype(f32)[None, :]
    full, diag, masks_d = _sweep_tables(S, CHUNK)
    _, _, masks_f = _sweep_tables(S, 1)
    full, diag = jnp.asarray(full), jnp.asarray(diag)
    mixd = _dattn_call(full, diag, jnp.asarray(masks_d), qd, kd, vd,
                       row(lam_q1), row(lam_k1), row(lam_q2), row(lam_k2), gcol)
    mixf = _fattn_call(full, diag, jnp.asarray(masks_f), qf, kf, vf)

    wo = w_out[0].astype(bf16)
    half = DIFF_HEADS * DIFF_VD
    out = _ffn_call(
        x.reshape(B * S, D), mixd.reshape(B * S, half), mixf.reshape(B * S, half),
        wo[:half], wo[half:], norm2_g[0][None, :], w_gate[0].astype(bf16), w_up[0].astype(bf16),
        w_down[0].astype(bf16), normf_g[None, :])
    return out.reshape(B, S, D)
```

```python
import functools
import math

import jax
import jax.numpy as jnp
import numpy as np
from jax import lax
from jax.experimental import pallas as pl
from jax.experimental.pallas import tpu as pltpu

D_MODEL = 1024
CHUNK = 64
DIFF_HEADS = 4
DIFF_DH = 64
DIFF_VD = 128
FOX_HEADS = 8
FOX_DH = 64
ROPE_THETA = 500000.0
ROPE_DIM = 16
ROPE_HALF = 8
D_FF = 2816
EPS = 1e-5
NEG = -1e30
LAM_INIT = 0.8 - 0.6 * math.exp(-0.3 * 0)
LOG2E = 1.4426950408889634

LANES = 128
MXU_N = 256
VMEM_LIMIT_BYTES = 56 * 1024 * 1024

PROJ_TM = 512
ATT_TQ = MXU_N
ATT_TK = 512
ATT_GROUP = 2
ATT_UNROLL = 2
ATT_FIN_UNROLL = 4
ATT_ONES_ROWS = 16
FFN_TM = 512

NAT_DK = DIFF_HEADS * 2 * DIFF_DH
NAT_FK = FOX_HEADS * FOX_DH
NAT_W = NAT_DK + NAT_FK
TR_DQ = DIFF_HEADS * 2 * DIFF_DH
TR_DV = DIFF_HEADS * DIFF_VD
TR_FQ = FOX_HEADS * FOX_DH
TR_FV = FOX_HEADS * FOX_DH
TR_FG = 16
TR_W = TR_DQ + TR_DV + TR_FQ + TR_FV + TR_FG
AUG_ONES = FOX_DH
AUG_BIAS = FOX_DH + 3
AUG_ROWS = 16
KF_W = FOX_HEADS * LANES


def _nt_dot(a, b):
    return lax.dot_general(a, b, (((1,), (1,)), ((), ())), preferred_element_type=jnp.float32)


def _split3(x):
    hi = x.astype(jnp.bfloat16)
    r = x - hi.astype(jnp.float32)
    mid = r.astype(jnp.bfloat16)
    lo = (r - mid.astype(jnp.float32)).astype(jnp.bfloat16)
    return hi, mid, lo


def _log_sigmoid(x):
    return jnp.minimum(x, 0.0) - jnp.log1p(jnp.exp(-jnp.abs(x)))


def _store_tiles(ref, x, rows, cols):
    for hd in range(x.shape[0] // rows):
        for c in range(x.shape[1] // cols):
            ref[0, hd, c] = x[hd * rows:(hd + 1) * rows, c * cols:(c + 1) * cols]


def _proj_kernel(x_ref, g_ref, wnat_ref, wtr_ref, bft_ref, cn_ref, s1n_ref, s2n_ref, ct_ref, st_ref,
                 tri_ref, en_ref, pt_ref, onesn_ref,
                 kd_ref, kf_ref, qd_ref, vd_ref, qf_ref, vf_ref, carry_t):
    @pl.when(pl.program_id(1) == 0)
    def _():
        carry_t[...] = jnp.zeros_like(carry_t)

    f32, bf16 = jnp.float32, jnp.bfloat16
    tm = x_ref.shape[1]
    x = x_ref[0]
    ms = jnp.mean(x * x, axis=-1, keepdims=True)
    h = (x * lax.rsqrt(ms + EPS) * g_ref[...]).astype(bf16)

    nat = jnp.dot(h, wnat_ref[...], preferred_element_type=f32)
    tr = _nt_dot(wtr_ref[...], h)

    cn, s1n, s2n = cn_ref[...], s1n_ref[...], s2n_ref[...]
    for hd in range(DIFF_HEADS):
        t = nat[:, hd * LANES:(hd + 1) * LANES]
        rot = t * cn + pltpu.roll(t, LANES - ROPE_HALF, 1) * s1n + pltpu.roll(t, ROPE_HALF, 1) * s2n
        kd_ref[0, hd] = rot.astype(bf16)

    ct, st = ct_ref[...], st_ref[...]
    pieces = []
    for comp in range(DIFF_HEADS * 2):
        base = comp * DIFF_DH
        t1 = tr[base:base + ROPE_HALF]
        t2 = tr[base + ROPE_HALF:base + ROPE_DIM]
        pieces += [t1 * ct - t2 * st, t2 * ct + t1 * st, tr[base + ROPE_DIM:base + DIFF_DH]]
    qd = jnp.concatenate(pieces, axis=0) * (DIFF_DH ** -0.5 * LOG2E)
    _store_tiles(qd_ref, qd.astype(bf16), 2 * DIFF_DH, ATT_TQ)
    _store_tiles(vd_ref, tr[TR_DQ:TR_DQ + TR_DV].astype(bf16), DIFF_VD, ATT_TK)

    o_fq = TR_DQ + TR_DV
    o_fv = o_fq + TR_FQ
    o_fg = o_fv + TR_FV
    lf_t = _log_sigmoid(tr[o_fg:o_fg + TR_FG] + bft_ref[...])
    c3t = _nt_dot(jnp.concatenate(_split3(lf_t), axis=0), tri_ref[...])
    cum_t = (c3t[2 * TR_FG:] + c3t[TR_FG:2 * TR_FG]) + c3t[:TR_FG] + carry_t[:, 0:1]
    carry_t[...] = jnp.broadcast_to(cum_t[:, tm - 1:tm], carry_t.shape)

    qb = jnp.concatenate(_split3(cum_t * LOG2E) + (jnp.ones((TR_FG, tm), bf16),), axis=0)
    aug_q = jnp.dot(pt_ref[...], qb, preferred_element_type=f32)
    fq = tr[o_fq:o_fq + TR_FQ] * (FOX_DH ** -0.5 * LOG2E)
    pad_rows = jnp.zeros((LANES - FOX_DH - AUG_ROWS, tm), f32)
    for hd in range(FOX_HEADS):
        blk = jnp.concatenate([fq[hd * FOX_DH:(hd + 1) * FOX_DH], aug_q[hd * AUG_ROWS:(hd + 1) * AUG_ROWS],
                               pad_rows], axis=0).astype(bf16)
        for c in range(tm // ATT_TQ):
            qf_ref[0, hd, c] = blk[:, c * ATT_TQ:(c + 1) * ATT_TQ]
    _store_tiles(vf_ref, tr[o_fv:o_fv + TR_FV].astype(bf16), FOX_DH, ATT_TK)

    parts = [part.astype(f32)[:FOX_HEADS] for part in _split3(cum_t * (-LOG2E))]
    parts.append(jnp.zeros((LANES - 3 * FOX_HEADS, tm), f32))
    kb = jnp.concatenate(parts, axis=0).T.astype(bf16)
    aug_k = jnp.dot(kb, en_ref[...], preferred_element_type=f32) + onesn_ref[...]
    lane = lax.broadcasted_iota(jnp.int32, (tm, LANES), 1)
    for hd in range(FOX_HEADS):
        src = nat[:, NAT_DK + (hd // 2) * LANES:NAT_DK + (hd // 2 + 1) * LANES]
        if hd % 2:
            src = pltpu.roll(src, FOX_DH, 1)
        kf_ref[0, hd] = jnp.where(lane < FOX_DH, src, aug_k[:, hd * LANES:(hd + 1) * LANES]).astype(bf16)


def _proj_call(x, g1, wnat, wtr, bft, cn, s1n, s2n, ct, st, tri, en, pt, onesn):
    B, S, D = x.shape
    tm = PROJ_TM
    const = lambda shape: pl.BlockSpec(shape, lambda b, i: (0,) * len(shape))
    tiles = lambda heads, rows, cols: pl.BlockSpec((1, heads, tm // cols, rows, cols), lambda b, i: (b, 0, i, 0, 0))
    bf16 = jnp.bfloat16
    return pl.pallas_call(
        _proj_kernel,
        grid=(B, S // tm),
        in_specs=[
            pl.BlockSpec((1, tm, D), lambda b, i: (b, i, 0)),
            const((1, D)),
            const((D, NAT_W)),
            const((TR_W, D)),
            const((TR_FG, tm)),
            pl.BlockSpec((tm, LANES), lambda b, i: (i, 0)),
            pl.BlockSpec((tm, LANES), lambda b, i: (i, 0)),
            pl.BlockSpec((tm, LANES), lambda b, i: (i, 0)),
            pl.BlockSpec((ROPE_HALF, tm), lambda b, i: (0, i)),
            pl.BlockSpec((ROPE_HALF, tm), lambda b, i: (0, i)),
            const((tm, tm)),
            const((LANES, KF_W)),
            const((FOX_HEADS * AUG_ROWS, 4 * TR_FG)),
            const((1, KF_W)),
        ],
        out_specs=[
            pl.BlockSpec((1, DIFF_HEADS, tm, LANES), lambda b, i: (b, 0, i, 0)),
            pl.BlockSpec((1, FOX_HEADS, tm, LANES), lambda b, i: (b, 0, i, 0)),
            tiles(DIFF_HEADS, 2 * DIFF_DH, ATT_TQ),
            tiles(DIFF_HEADS, DIFF_VD, ATT_TK),
            tiles(FOX_HEADS, LANES, ATT_TQ),
            tiles(FOX_HEADS, FOX_DH, ATT_TK),
        ],
        out_shape=[
            jax.ShapeDtypeStruct((B, DIFF_HEADS, S, LANES), bf16),
            jax.ShapeDtypeStruct((B, FOX_HEADS, S, LANES), bf16),
            jax.ShapeDtypeStruct((B, DIFF_HEADS, S // ATT_TQ, 2 * DIFF_DH, ATT_TQ), bf16),
            jax.ShapeDtypeStruct((B, DIFF_HEADS, S // ATT_TK, DIFF_VD, ATT_TK), bf16),
            jax.ShapeDtypeStruct((B, FOX_HEADS, S // ATT_TQ, LANES, ATT_TQ), bf16),
            jax.ShapeDtypeStruct((B, FOX_HEADS, S // ATT_TK, FOX_DH, ATT_TK), bf16),
        ],
        scratch_shapes=[pltpu.VMEM((TR_FG, LANES), jnp.float32)],
        compiler_params=pltpu.CompilerParams(
            dimension_semantics=("arbitrary", "arbitrary"), vmem_limit_bytes=VMEM_LIMIT_BYTES),
        name="proj",
    )(x, g1, wnat, wtr, bft, cn, s1n, s2n, ct, st, tri, en, pt, onesn)


def _sweep_tables(S, chunk):
    nq, ratio = S // ATT_TQ, ATT_TK // ATT_TQ
    diag = [(i, i // ratio, i % ratio) for i in range(nq)]
    full = [(i, j, 0) for i in range(nq) for j in range(i // ratio)]
    assert len(full) % (2 * ATT_GROUP) == 0 and len(diag) % (2 * ATT_GROUP) == 0
    kk = np.arange(ATT_TK)[:, None] // chunk
    masks = []
    for r in range(ratio):
        qq = (r * ATT_TQ + np.arange(ATT_TQ))[None, :] // chunk
        masks.append(np.where(kk <= qq, 0.0, NEG).astype(np.float32))
    return np.array(full + diag, np.int32).T.copy(), len(full), np.stack(masks)


def _init_sweep_state(m_s, acc_s):
    m_s[...] = jnp.full_like(m_s, NEG)
    acc_s[...] = jnp.zeros_like(acc_s)


def _with_ones_rows(v_t):
    return jnp.concatenate([v_t, jnp.ones((ATT_ONES_ROWS, v_t.shape[1]), v_t.dtype)], axis=0)


def _flat_sweep(tab_ref, mask_ref, n_plain, n_heads, scores, values, m_s, acc_s, sbuf, xbuf, pbuf, abuf):
    n_groups, n_plain_groups = tab_ref.shape[1] // ATT_GROUP, n_plain // ATT_GROUP
    assert n_groups % 2 == 0 and n_plain_groups % 2 == 0 and 2 <= n_plain_groups <= n_groups

    def slot(par, a):
        return par * ATT_GROUP + a

    def stage_scores(u, par, masked):
        for a in range(ATT_GROUP):
            t = u * ATT_GROUP + a
            it, jt = tab_ref[0, t], tab_ref[1, t]
            for h in range(n_heads):
                s = scores(h, it, jt)
                if masked:
                    s = s + mask_ref[tab_ref[2, t]]
                sbuf[slot(par, a), h] = s
                xbuf[par][a, h] = jnp.max(s, axis=0, keepdims=True)

    def stage_softmax(u, par):
        for a in range(ATT_GROUP):
            it = tab_ref[0, u * ATT_GROUP + a]
            for h in range(n_heads):
                m_old = m_s[h, it]
                m_new = jnp.maximum(m_old, xbuf[par][a, h])
                alpha = jnp.exp2(m_old - m_new)
                p = jnp.exp2(sbuf[slot(par, a), h] - m_new)
                m_s[h, it] = m_new
                pbuf[slot(par, a), h] = p.astype(jnp.bfloat16)
                abuf[par][a, h] = alpha

    def stage_values(u, par):
        for a in range(ATT_GROUP):
            t = u * ATT_GROUP + a
            it, jt = tab_ref[0, t], tab_ref[1, t]
            for h in range(n_heads):
                pv = jnp.dot(_with_ones_rows(values(h, jt)), pbuf[slot(par, a), h],
                             preferred_element_type=jnp.float32)
                acc_s[h, it] = abuf[par][a, h] * acc_s[h, it] + pv

    def step(u, par, masked, with_values=True):
        stage_scores(u, par, masked)
        if with_values:
            stage_values(u - 2, par)
        stage_softmax(u - 1, 1 - par)

    def run(lo, hi, masked):
        first = lo + (hi - lo) % ATT_UNROLL
        for u in range(lo, first):
            step(u, u % 2, masked)

        def body(v, c):
            for k in range(ATT_UNROLL):
                step(first + ATT_UNROLL * v + k, (first + k) % 2, masked)
            return c

        lax.fori_loop(0, (hi - first) // ATT_UNROLL, body, 0)

    stage_scores(0, 0, False)
    step(1, 1, False, with_values=False)
    run(2, n_plain_groups, False)
    run(n_plain_groups, n_groups, True)
    stage_values(n_groups - 2, 0)
    stage_softmax(n_groups - 1, 1)
    stage_values(n_groups - 1, 1)


def _sweep_scratch(n_heads, nq, dv):
    f32 = jnp.float32
    slots = 2 * ATT_GROUP
    return [
        pltpu.VMEM((n_heads, nq, 1, ATT_TQ), f32),
        pltpu.VMEM((n_heads, nq, dv + ATT_ONES_ROWS, ATT_TQ), f32),
        pltpu.VMEM((slots, n_heads, ATT_TK, ATT_TQ), f32),
        pltpu.VMEM((slots, n_heads, ATT_TK, ATT_TQ), jnp.bfloat16),
    ] + [pltpu.VMEM((ATT_GROUP, n_heads, 1, ATT_TQ), f32)] * 4


def _for_each_query_block(nq, fn):
    def body(g, c):
        for k in range(ATT_FIN_UNROLL):
            fn(g * ATT_FIN_UNROLL + k, c)
        return c

    lax.fori_loop(0, nq // ATT_FIN_UNROLL, body, 0)


def _key_slice(jt):
    return pl.ds(pl.multiple_of(jt * ATT_TK, ATT_TK), ATT_TK)


def _query_slice(it):
    return pl.ds(pl.multiple_of(it * ATT_TQ, ATT_TQ), ATT_TQ)


def _dattn_kernel(n_plain, tab_ref, q_ref, k_ref, v_ref, mask_ref, lq1_ref, lk1_ref, lq2_ref, lk2_ref, g_ref,
                  o_ref, m_s, acc_s, sbuf, pbuf, x0, x1, a0, a1):
    nq = m_s.shape[1]
    xbuf, abuf = (x0, x1), (a0, a1)
    row = lax.broadcasted_iota(jnp.int32, (2 * DIFF_DH, ATT_TQ), 0)

    def scores(h, it, jt):
        q = q_ref[0, 0, it]
        keep = (row < DIFF_DH) if h == 0 else (row >= DIFF_DH)
        rhs = jnp.where(keep, q, jnp.zeros_like(q))
        return jnp.dot(k_ref[0, 0, _key_slice(jt), :], rhs, preferred_element_type=jnp.float32)

    def values(h, jt):
        return v_ref[0, 0, jt]

    _init_sweep_state(m_s, acc_s)
    _flat_sweep(tab_ref, mask_ref, n_plain, 2, scores, values, m_s, acc_s, sbuf, xbuf, pbuf, abuf)

    lam = (jnp.exp(jnp.sum(lq1_ref[...] * lk1_ref[...], axis=-1, keepdims=True))
           - jnp.exp(jnp.sum(lq2_ref[...] * lk2_ref[...], axis=-1, keepdims=True)) + LAM_INIT)
    gain = g_ref[...] * (1.0 - LAM_INIT)

    def finalize(it, c):
        a1, a2 = acc_s[0, it], acc_s[1, it]
        o = (a1[:DIFF_VD] / a1[DIFF_VD:DIFF_VD + 1]
             - lam * (a2[:DIFF_VD] / a2[DIFF_VD:DIFF_VD + 1]))
        ms = jnp.mean(o * o, axis=0, keepdims=True)
        o = o * lax.rsqrt(ms + EPS) * gain
        o_ref[0, _query_slice(it), :] = o.T.astype(o_ref.dtype)
        return c

    _for_each_query_block(nq, finalize)


def _dattn_call(tab, n_plain, masks, qd, kd, vd, lq1, lk1, lq2, lk2, gcol):
    B, _, nq = qd.shape[:3]
    S = nq * ATT_TQ
    vec = lambda: pl.BlockSpec((1, DIFF_DH), lambda b, h, *_: (0, 0))
    return pl.pallas_call(
        functools.partial(_dattn_kernel, n_plain),
        grid_spec=pltpu.PrefetchScalarGridSpec(
            num_scalar_prefetch=1,
            grid=(B, DIFF_HEADS),
            in_specs=[
                pl.BlockSpec((1, 1) + qd.shape[2:], lambda b, h, *_: (b, h, 0, 0, 0)),
                pl.BlockSpec((1, 1, S, LANES), lambda b, h, *_: (b, h, 0, 0)),
                pl.BlockSpec((1, 1) + vd.shape[2:], lambda b, h, *_: (b, h, 0, 0, 0)),
                pl.BlockSpec(masks.shape, lambda b, h, *_: (0, 0, 0)),
                vec(), vec(), vec(), vec(),
                pl.BlockSpec((DIFF_VD, ATT_TQ), lambda b, h, *_: (0, 0)),
            ],
            out_specs=pl.BlockSpec((1, S, DIFF_VD), lambda b, h, *_: (b, 0, h)),
            scratch_shapes=_sweep_scratch(2, nq, DIFF_VD),
        ),
        out_shape=jax.ShapeDtypeStruct((B, S, DIFF_HEADS * DIFF_VD), jnp.bfloat16),
        compiler_params=pltpu.CompilerParams(
            dimension_semantics=("arbitrary", "arbitrary"), vmem_limit_bytes=VMEM_LIMIT_BYTES),
        name="dattn",
    )(tab, qd, kd, vd, masks, lq1, lk1, lq2, lk2, gcol)


def _fattn_kernel(n_plain, tab_ref, q_ref, k_ref, v_ref, mask_ref, o_ref, m_s, acc_s, sbuf, pbuf, x0, x1, a0, a1):
    nq = m_s.shape[1]
    xbuf, abuf = (x0, x1), (a0, a1)

    def scores(h, it, jt):
        q = q_ref[0, h, it]
        return jnp.dot(k_ref[0, h, _key_slice(jt), :], q, preferred_element_type=jnp.float32)

    def values(h, jt):
        return v_ref[0, h, jt]

    _init_sweep_state(m_s, acc_s)
    _flat_sweep(tab_ref, mask_ref, n_plain, 2, scores, values, m_s, acc_s, sbuf, xbuf, pbuf, abuf)

    def finalize(it, c):
        heads = [acc_s[h, it] for h in range(2)]
        o = jnp.concatenate([a[:FOX_DH] / a[FOX_DH:FOX_DH + 1] for a in heads], axis=0)
        o_ref[0, _query_slice(it), :] = o.T.astype(o_ref.dtype)
        return c

    _for_each_query_block(nq, finalize)


def _fattn_call(tab, n_plain, masks, qf, kf, vf):
    B, _, nq = qf.shape[:3]
    S = nq * ATT_TQ
    return pl.pallas_call(
        functools.partial(_fattn_kernel, n_plain),
        grid_spec=pltpu.PrefetchScalarGridSpec(
            num_scalar_prefetch=1,
            grid=(B, FOX_HEADS // 2),
            in_specs=[
                pl.BlockSpec((1, 2) + qf.shape[2:], lambda b, p, *_: (b, p, 0, 0, 0)),
                pl.BlockSpec((1, 2, S, LANES), lambda b, p, *_: (b, p, 0, 0)),
                pl.BlockSpec((1, 2) + vf.shape[2:], lambda b, p, *_: (b, p, 0, 0, 0)),
                pl.BlockSpec(masks.shape, lambda b, p, *_: (0, 0, 0)),
            ],
            out_specs=pl.BlockSpec((1, S, 2 * FOX_DH), lambda b, p, *_: (b, 0, p)),
            scratch_shapes=_sweep_scratch(2, nq, FOX_DH),
        ),
        out_shape=jax.ShapeDtypeStruct((B, S, FOX_HEADS * FOX_DH), jnp.bfloat16),
        compiler_params=pltpu.CompilerParams(
            dimension_semantics=("arbitrary", "arbitrary"), vmem_limit_bytes=VMEM_LIMIT_BYTES),
        name="fattn",
    )(tab, qf, kf, vf, masks)


def _ffn_kernel(x_ref, md_ref, mf_ref, wod_ref, wof_ref, g2_ref, wg_ref, wu_ref, wd_ref, gf_ref, o_ref):
    x1 = (x_ref[...]
          + jnp.dot(md_ref[...], wod_ref[...], preferred_element_type=jnp.float32)
          + jnp.dot(mf_ref[...], wof_ref[...], preferred_element_type=jnp.float32))
    ms = jnp.mean(x1 * x1, axis=-1, keepdims=True)
    h2 = (x1 * lax.rsqrt(ms + EPS) * g2_ref[...]).astype(jnp.bfloat16)
    g = jnp.dot(h2, wg_ref[...], preferred_element_type=jnp.float32)
    u = jnp.dot(h2, wu_ref[...], preferred_element_type=jnp.float32)
    act = (g * jax.nn.sigmoid(g) * u).astype(jnp.bfloat16)
    y = x1 + jnp.dot(act, wd_ref[...], preferred_element_type=jnp.float32)
    ms = jnp.mean(y * y, axis=-1, keepdims=True)
    o_ref[...] = y * lax.rsqrt(ms + EPS) * gf_ref[...]


def _ffn_call(x2, md, mf, wod, wof, g2, wg, wu, wd, gf):
    n, D = x2.shape
    tm = FFN_TM
    half = md.shape[1]
    resident = lambda shape: pl.BlockSpec(shape, lambda i: (0, 0), pipeline_mode=pl.Buffered(1))
    return pl.pallas_call(
        _ffn_kernel,
        grid=(n // tm,),
        in_specs=[
            pl.BlockSpec((tm, D), lambda i: (i, 0)),
            pl.BlockSpec((tm, half), lambda i: (i, 0)),
            pl.BlockSpec((tm, half), lambda i: (i, 0)),
            resident((half, D)),
            resident((half, D)),
            resident((1, D)),
            resident((D, D_FF)),
            resident((D, D_FF)),
            resident((D_FF, D)),
            resident((1, D)),
        ],
        out_specs=pl.BlockSpec((tm, D), lambda i: (i, 0)),
        out_shape=jax.ShapeDtypeStruct((n, D), jnp.float32),
        compiler_params=pltpu.CompilerParams(
            dimension_semantics=("arbitrary",), vmem_limit_bytes=VMEM_LIMIT_BYTES),
        name="ffn",
    )(x2, md, mf, wod, wof, g2, wg, wu, wd, gf)


def _placement_tables():
    en = np.zeros((LANES, KF_W), np.float32)
    pt = np.zeros((FOX_HEADS * AUG_ROWS, 4 * TR_FG), np.float32)
    onesn = np.zeros((1, KF_W), np.float32)
    for h in range(FOX_HEADS):
        for p in range(3):
            en[p * FOX_HEADS + h, h * LANES + AUG_BIAS + p] = 1.0
            onesn[0, h * LANES + AUG_ONES + p] = 1.0
            pt[h * AUG_ROWS + (AUG_ONES - FOX_DH) + p, p * TR_FG + h] = 1.0
            pt[h * AUG_ROWS + (AUG_BIAS - FOX_DH) + p, 3 * TR_FG] = 1.0
    tri = np.tril(np.ones((PROJ_TM, PROJ_TM), np.float32))
    return en, pt, onesn, tri


def _rope_tables(S):
    f32 = jnp.float32
    pos = jnp.arange(S, dtype=jnp.int32)
    inv_freq = ROPE_THETA ** (-jnp.arange(0, ROPE_DIM, 2, dtype=f32) / ROPE_DIM)
    ang = pos.astype(f32)[:, None] * inv_freq[None, :]
    cos, sin = jnp.cos(ang), jnp.sin(ang)
    lane = np.arange(LANES) % DIFF_DH
    sel = lane % ROPE_HALF
    lo = jnp.asarray(lane < ROPE_HALF)[None, :]
    hi = jnp.asarray((lane >= ROPE_HALF) & (lane < ROPE_DIM))[None, :]
    cos_l, sin_l = cos[:, sel], sin[:, sel]
    cn = jnp.where(lo | hi, cos_l, 1.0)
    s1n = jnp.where(lo, -sin_l, 0.0)
    s2n = jnp.where(hi, sin_l, 0.0)
    return cn, s1n, s2n, cos.T, sin.T


def kernel(x, norm1_g, w_in, b_f, lam_q1, lam_k1, lam_q2, lam_k2, subln_g, w_out, norm2_g, w_gate, w_up,
           w_down, normf_g):
    f32, bf16 = jnp.float32, jnp.bfloat16
    B, S, D = x.shape
    en, pt, onesn, tri = _placement_tables()
    cn, s1n, s2n, ct, st = _rope_tables(S)

    w = w_in[0]
    o_dq, o_dk, o_dv, o_fq, o_fk, o_fv, o_fg = 0, 512, 1024, 1536, 2048, 2560, 3072
    wnat = jnp.concatenate([w[:, o_dk:o_dv], w[:, o_fk:o_fv]], axis=1).astype(bf16)
    wtr = jnp.concatenate(
        [w[:, o_dq:o_dk], w[:, o_dv:o_fq], w[:, o_fq:o_fk], w[:, o_fv:o_fg],
         jnp.pad(w[:, o_fg:], ((0, 0), (0, TR_FG - FOX_HEADS)))], axis=1).T.astype(bf16)
    bft = jnp.broadcast_to(jnp.pad(b_f[0].astype(f32), (0, TR_FG - FOX_HEADS))[:, None], (TR_FG, PROJ_TM))

    kd, kf, qd, vd, qf, vf = _proj_call(
        x, norm1_g[0][None, :], wnat, wtr, bft, cn, s1n, s2n, ct, st,
        jnp.asarray(tri, bf16), jnp.asarray(en, bf16), jnp.asarray(pt, bf16), jnp.asarray(onesn, f32))

    gcol = jnp.broadcast_to(subln_g[0].astype(f32)[:, None], (DIFF_VD, ATT_TQ))
    row = lambda v: v[0].astype(f32)[None, :]
    tab, n_plain, masks_d = _sweep_tables(S, CHUNK)
    _, _, masks_f = _sweep_tables(S, 1)
    tab = jnp.asarray(tab)
    mixd = _dattn_call(tab, n_plain, jnp.asarray(masks_d), qd, kd, vd,
                       row(lam_q1), row(lam_k1), row(lam_q2), row(lam_k2), gcol)
    mixf = _fattn_call(tab, n_plain, jnp.asarray(masks_f), qf, kf, vf)

    wo = w_out[0].astype(bf16)
    half = DIFF_HEADS * DIFF_VD
    out = _ffn_call(
        x.reshape(B * S, D), mixd.reshape(B * S, half), mixf.reshape(B * S, half),
        wo[:half], wo[half:], norm2_g[0][None, :], w_gate[0].astype(bf16), w_up[0].astype(bf16),
        w_down[0].astype(bf16), normf_g[None, :])
    return out.reshape(B, S, D)
```

```python
import functools
import math

import jax
import jax.numpy as jnp
import numpy as np
from jax import lax
from jax.experimental import pallas as pl
from jax.experimental.pallas import tpu as pltpu

D_MODEL = 1024
CHUNK = 64
DIFF_HEADS = 4
DIFF_DH = 64
DIFF_VD = 128
FOX_HEADS = 8
FOX_DH = 64
ROPE_THETA = 500000.0
ROPE_DIM = 16
ROPE_HALF = 8
D_FF = 2816
EPS = 1e-5
NEG = -1e30
LAM_INIT = 0.8 - 0.6 * math.exp(-0.3 * 0)
LOG2E = 1.4426950408889634

LANES = 128
MXU_N = 256
VMEM_LIMIT_BYTES = 56 * 1024 * 1024

PROJ_TM = 512
ATT_TQ = MXU_N
ATT_TK = 512
ATT_GROUP = 2
ATT_UNROLL = 2
ATT_FIN_UNROLL = 4
ATT_PAD_ROWS = 16
ATT_ONES_ROWS = 16
FFN_TM = 512

NAT_DK = DIFF_HEADS * 2 * DIFF_DH
NAT_FK = FOX_HEADS * FOX_DH
NAT_W = NAT_DK + NAT_FK
TR_DQ = DIFF_HEADS * 2 * DIFF_DH
TR_DV = DIFF_HEADS * DIFF_VD
TR_FQ = FOX_HEADS * FOX_DH
TR_FV = FOX_HEADS * FOX_DH
TR_FG = 16
TR_W = TR_DQ + TR_DV + TR_FQ + TR_FV + TR_FG
AUG_ONES = FOX_DH
AUG_BIAS = FOX_DH + 3
AUG_ROWS = 16
KF_W = FOX_HEADS * LANES


def _nt_dot(a, b):
    return lax.dot_general(a, b, (((1,), (1,)), ((), ())), preferred_element_type=jnp.float32)


def _split3(x):
    hi = x.astype(jnp.bfloat16)
    r = x - hi.astype(jnp.float32)
    mid = r.astype(jnp.bfloat16)
    lo = (r - mid.astype(jnp.float32)).astype(jnp.bfloat16)
    return hi, mid, lo


def _log_sigmoid(x):
    return jnp.minimum(x, 0.0) - jnp.log1p(jnp.exp(-jnp.abs(x)))


def _store_tiles(ref, x, rows, cols):
    for hd in range(x.shape[0] // rows):
        for c in range(x.shape[1] // cols):
            ref[0, hd, c] = x[hd * rows:(hd + 1) * rows, c * cols:(c + 1) * cols]


def _proj_kernel(x_ref, g_ref, wnat_ref, wtr_ref, bft_ref, cn_ref, s1n_ref, s2n_ref, ct_ref, st_ref,
                 tri_ref, en_ref, pt_ref, onesn_ref,
                 kd_ref, kf_ref, qd_ref, vd_ref, qf_ref, vf_ref, carry_t):
    @pl.when(pl.program_id(1) == 0)
    def _():
        carry_t[...] = jnp.zeros_like(carry_t)

    f32, bf16 = jnp.float32, jnp.bfloat16
    tm = x_ref.shape[1]
    x = x_ref[0]
    ms = jnp.mean(x * x, axis=-1, keepdims=True)
    h = (x * lax.rsqrt(ms + EPS) * g_ref[...]).astype(bf16)

    nat = jnp.dot(h, wnat_ref[...], preferred_element_type=f32)
    tr = _nt_dot(wtr_ref[...], h)

    cn, s1n, s2n = cn_ref[...], s1n_ref[...], s2n_ref[...]
    for hd in range(DIFF_HEADS):
        t = nat[:, hd * LANES:(hd + 1) * LANES]
        rot = t * cn + pltpu.roll(t, LANES - ROPE_HALF, 1) * s1n + pltpu.roll(t, ROPE_HALF, 1) * s2n
        kd_ref[0, hd] = rot.astype(bf16)

    ct, st = ct_ref[...], st_ref[...]
    pieces = []
    for comp in range(DIFF_HEADS * 2):
        base = comp * DIFF_DH
        t1 = tr[base:base + ROPE_HALF]
        t2 = tr[base + ROPE_HALF:base + ROPE_DIM]
        pieces += [t1 * ct - t2 * st, t2 * ct + t1 * st, tr[base + ROPE_DIM:base + DIFF_DH]]
    qd = jnp.concatenate(pieces, axis=0) * (DIFF_DH ** -0.5 * LOG2E)
    _store_tiles(qd_ref, qd.astype(bf16), 2 * DIFF_DH, ATT_TQ)
    _store_tiles(vd_ref, tr[TR_DQ:TR_DQ + TR_DV].astype(bf16), DIFF_VD, ATT_TK)

    o_fq = TR_DQ + TR_DV
    o_fv = o_fq + TR_FQ
    o_fg = o_fv + TR_FV
    lf_t = _log_sigmoid(tr[o_fg:o_fg + TR_FG] + bft_ref[...])
    c3t = _nt_dot(jnp.concatenate(_split3(lf_t), axis=0), tri_ref[...])
    cum_t = (c3t[2 * TR_FG:] + c3t[TR_FG:2 * TR_FG]) + c3t[:TR_FG] + carry_t[:, 0:1]
    carry_t[...] = jnp.broadcast_to(cum_t[:, tm - 1:tm], carry_t.shape)

    qb = jnp.concatenate(_split3(cum_t * LOG2E) + (jnp.ones((TR_FG, tm), bf16),), axis=0)
    aug_q = jnp.dot(pt_ref[...], qb, preferred_element_type=f32)
    fq = tr[o_fq:o_fq + TR_FQ] * (FOX_DH ** -0.5 * LOG2E)
    pad_rows = jnp.zeros((LANES - FOX_DH - AUG_ROWS, tm), f32)
    for hd in range(FOX_HEADS):
        blk = jnp.concatenate([fq[hd * FOX_DH:(hd + 1) * FOX_DH], aug_q[hd * AUG_ROWS:(hd + 1) * AUG_ROWS],
                               pad_rows], axis=0).astype(bf16)
        for c in range(tm // ATT_TQ):
            qf_ref[0, hd, c] = blk[:, c * ATT_TQ:(c + 1) * ATT_TQ]
    _store_tiles(vf_ref, tr[o_fv:o_fv + TR_FV].astype(bf16), FOX_DH, ATT_TK)

    parts = [part.astype(f32)[:FOX_HEADS] for part in _split3(cum_t * (-LOG2E))]
    parts.append(jnp.zeros((LANES - 3 * FOX_HEADS, tm), f32))
    kb = jnp.concatenate(parts, axis=0).T.astype(bf16)
    aug_k = jnp.dot(kb, en_ref[...], preferred_element_type=f32) + onesn_ref[...]
    lane = lax.broadcasted_iota(jnp.int32, (tm, LANES), 1)
    for hd in range(FOX_HEADS):
        src = nat[:, NAT_DK + (hd // 2) * LANES:NAT_DK + (hd // 2 + 1) * LANES]
        if hd % 2:
            src = pltpu.roll(src, FOX_DH, 1)
        kf_ref[0, hd] = jnp.where(lane < FOX_DH, src, aug_k[:, hd * LANES:(hd + 1) * LANES]).astype(bf16)


def _proj_call(x, g1, wnat, wtr, bft, cn, s1n, s2n, ct, st, tri, en, pt, onesn):
    B, S, D = x.shape
    tm = PROJ_TM
    const = lambda shape: pl.BlockSpec(shape, lambda b, i: (0,) * len(shape))
    tiles = lambda heads, rows, cols: pl.BlockSpec((1, heads, tm // cols, rows, cols), lambda b, i: (b, 0, i, 0, 0))
    bf16 = jnp.bfloat16
    return pl.pallas_call(
        _proj_kernel,
        grid=(B, S // tm),
        in_specs=[
            pl.BlockSpec((1, tm, D), lambda b, i: (b, i, 0)),
            const((1, D)),
            const((D, NAT_W)),
            const((TR_W, D)),
            const((TR_FG, tm)),
            pl.BlockSpec((tm, LANES), lambda b, i: (i, 0)),
            pl.BlockSpec((tm, LANES), lambda b, i: (i, 0)),
            pl.BlockSpec((tm, LANES), lambda b, i: (i, 0)),
            pl.BlockSpec((ROPE_HALF, tm), lambda b, i: (0, i)),
            pl.BlockSpec((ROPE_HALF, tm), lambda b, i: (0, i)),
            const((tm, tm)),
            const((LANES, KF_W)),
            const((FOX_HEADS * AUG_ROWS, 4 * TR_FG)),
            const((1, KF_W)),
        ],
        out_specs=[
            pl.BlockSpec((1, DIFF_HEADS, tm, LANES), lambda b, i: (b, 0, i, 0)),
            pl.BlockSpec((1, FOX_HEADS, tm, LANES), lambda b, i: (b, 0, i, 0)),
            tiles(DIFF_HEADS, 2 * DIFF_DH, ATT_TQ),
            tiles(DIFF_HEADS, DIFF_VD, ATT_TK),
            tiles(FOX_HEADS, LANES, ATT_TQ),
            tiles(FOX_HEADS, FOX_DH, ATT_TK),
        ],
        out_shape=[
            jax.ShapeDtypeStruct((B, DIFF_HEADS, S, LANES), bf16),
            jax.ShapeDtypeStruct((B, FOX_HEADS, S, LANES), bf16),
            jax.ShapeDtypeStruct((B, DIFF_HEADS, S // ATT_TQ, 2 * DIFF_DH, ATT_TQ), bf16),
            jax.ShapeDtypeStruct((B, DIFF_HEADS, S // ATT_TK, DIFF_VD, ATT_TK), bf16),
            jax.ShapeDtypeStruct((B, FOX_HEADS, S // ATT_TQ, LANES, ATT_TQ), bf16),
            jax.ShapeDtypeStruct((B, FOX_HEADS, S // ATT_TK, FOX_DH, ATT_TK), bf16),
        ],
        scratch_shapes=[pltpu.VMEM((TR_FG, LANES), jnp.float32)],
        compiler_params=pltpu.CompilerParams(
            dimension_semantics=("arbitrary", "arbitrary"), vmem_limit_bytes=VMEM_LIMIT_BYTES),
        name="proj",
    )(x, g1, wnat, wtr, bft, cn, s1n, s2n, ct, st, tri, en, pt, onesn)


def _sweep_tables(S, chunk):
    nq, ratio = S // ATT_TQ, ATT_TK // ATT_TQ
    diag = [(i, i // ratio, i % ratio) for i in range(nq)]
    full = [(i, j, 0) for i in range(nq) for j in range(i // ratio)]
    assert len(full) % (2 * ATT_GROUP) == 0 and len(diag) % (2 * ATT_GROUP) == 0
    kk = np.arange(ATT_TK)[:, None] // chunk
    masks = []
    for r in range(ratio):
        qq = (r * ATT_TQ + np.arange(ATT_TQ))[None, :] // chunk
        masks.append(np.where(kk <= qq, 0.0, NEG).astype(np.float32))
    return np.array(full + diag, np.int32).T.copy(), len(full), np.stack(masks)


def _init_sweep_state(m_s, acc_s):
    m_s[...] = jnp.full_like(m_s, NEG)
    acc_s[...] = jnp.zeros_like(acc_s)


def _with_ones_rows(v_t):
    return jnp.concatenate([v_t, jnp.ones((ATT_ONES_ROWS, v_t.shape[1]), v_t.dtype)], axis=0)


def _flat_sweep(tab_ref, mask_ref, n_plain, n_heads, scores, values, m_s, acc_s, sbuf, xbuf, pbuf, abuf):
    n_groups, n_plain_groups = tab_ref.shape[1] // ATT_GROUP, n_plain // ATT_GROUP
    assert n_groups % 2 == 0 and n_plain_groups % 2 == 0 and 2 <= n_plain_groups <= n_groups

    def slot(par, a):
        return par * ATT_GROUP + a

    def stage_scores(u, par, masked):
        for a in range(ATT_GROUP):
            t = u * ATT_GROUP + a
            it, jt = tab_ref[0, t], tab_ref[1, t]
            for h in range(n_heads):
                s = scores(h, it, jt)
                if masked:
                    s = s + mask_ref[tab_ref[2, t]]
                sbuf[slot(par, a), h, :ATT_TK] = s
                xbuf[par][a, h] = jnp.max(s, axis=0, keepdims=True)

    def stage_softmax(u, par):
        for a in range(ATT_GROUP):
            it = tab_ref[0, u * ATT_GROUP + a]
            for h in range(n_heads):
                m_old = m_s[h, it]
                m_new = jnp.maximum(m_old, xbuf[par][a, h])
                alpha = jnp.exp2(m_old - m_new)
                p = jnp.exp2(sbuf[slot(par, a), h, :ATT_TK] - m_new)
                m_s[h, it] = m_new
                pbuf[slot(par, a), h, :ATT_TK] = p.astype(jnp.bfloat16)
                abuf[par][a, h] = alpha

    def stage_values(u, par):
        for a in range(ATT_GROUP):
            t = u * ATT_GROUP + a
            it, jt = tab_ref[0, t], tab_ref[1, t]
            for h in range(n_heads):
                pv = jnp.dot(_with_ones_rows(values(h, jt)), pbuf[slot(par, a), h, :ATT_TK],
                             preferred_element_type=jnp.float32)
                acc_s[h, it] = abuf[par][a, h] * acc_s[h, it] + pv

    def step(u, par, masked, with_values=True):
        stage_scores(u, par, masked)
        if with_values:
            stage_values(u - 2, par)
        stage_softmax(u - 1, 1 - par)

    def run(lo, hi, masked):
        first = lo + (hi - lo) % ATT_UNROLL
        for u in range(lo, first):
            step(u, u % 2, masked)

        def body(v, c):
            for k in range(ATT_UNROLL):
                step(first + ATT_UNROLL * v + k, (first + k) % 2, masked)
            return c

        lax.fori_loop(0, (hi - first) // ATT_UNROLL, body, 0)

    stage_scores(0, 0, False)
    step(1, 1, False, with_values=False)
    run(2, n_plain_groups, False)
    run(n_plain_groups, n_groups, True)
    stage_values(n_groups - 2, 0)
    stage_softmax(n_groups - 1, 1)
    stage_values(n_groups - 1, 1)


def _sweep_scratch(n_heads, nq, dv):
    f32 = jnp.float32
    slots = 2 * ATT_GROUP
    return [
        pltpu.VMEM((n_heads, nq, 1, ATT_TQ), f32),
        pltpu.VMEM((n_heads, nq, dv + ATT_ONES_ROWS, ATT_TQ), f32),
        pltpu.VMEM((slots, n_heads, ATT_TK + ATT_PAD_ROWS // 2, ATT_TQ), f32),
        pltpu.VMEM((slots, n_heads, ATT_TK + ATT_PAD_ROWS, ATT_TQ), jnp.bfloat16),
    ] + [pltpu.VMEM((ATT_GROUP, n_heads, 1, ATT_TQ), f32)] * 4


def _for_each_query_block(nq, fn):
    def body(g, c):
        for k in range(ATT_FIN_UNROLL):
            fn(g * ATT_FIN_UNROLL + k, c)
        return c

    lax.fori_loop(0, nq // ATT_FIN_UNROLL, body, 0)


def _key_slice(jt):
    return pl.ds(pl.multiple_of(jt * ATT_TK, ATT_TK), ATT_TK)


def _query_slice(it):
    return pl.ds(pl.multiple_of(it * ATT_TQ, ATT_TQ), ATT_TQ)


def _dattn_kernel(n_plain, tab_ref, q_ref, k_ref, v_ref, mask_ref, lq1_ref, lk1_ref, lq2_ref, lk2_ref, g_ref,
                  o_ref, m_s, acc_s, sbuf, pbuf, x0, x1, a0, a1):
    nq = m_s.shape[1]
    xbuf, abuf = (x0, x1), (a0, a1)
    row = lax.broadcasted_iota(jnp.int32, (2 * DIFF_DH, ATT_TQ), 0)

    def scores(h, it, jt):
        q = q_ref[0, 0, it]
        keep = (row < DIFF_DH) if h == 0 else (row >= DIFF_DH)
        rhs = jnp.where(keep, q, jnp.zeros_like(q))
        return jnp.dot(k_ref[0, 0, _key_slice(jt), :], rhs, preferred_element_type=jnp.float32)

    def values(h, jt):
        return v_ref[0, 0, jt]

    _init_sweep_state(m_s, acc_s)
    _flat_sweep(tab_ref, mask_ref, n_plain, 2, scores, values, m_s, acc_s, sbuf, xbuf, pbuf, abuf)

    lam = (jnp.exp(jnp.sum(lq1_ref[...] * lk1_ref[...], axis=-1, keepdims=True))
           - jnp.exp(jnp.sum(lq2_ref[...] * lk2_ref[...], axis=-1, keepdims=True)) + LAM_INIT)
    gain = g_ref[...] * (1.0 - LAM_INIT)

    def finalize(it, c):
        a1, a2 = acc_s[0, it], acc_s[1, it]
        o = (a1[:DIFF_VD] / a1[DIFF_VD:DIFF_VD + 1]
             - lam * (a2[:DIFF_VD] / a2[DIFF_VD:DIFF_VD + 1]))
        ms = jnp.mean(o * o, axis=0, keepdims=True)
        o = o * lax.rsqrt(ms + EPS) * gain
        o_ref[0, _query_slice(it), :] = o.T.astype(o_ref.dtype)
        return c

    _for_each_query_block(nq, finalize)


def _dattn_call(tab, n_plain, masks, qd, kd, vd, lq1, lk1, lq2, lk2, gcol):
    B, _, nq = qd.shape[:3]
    S = nq * ATT_TQ
    vec = lambda: pl.BlockSpec((1, DIFF_DH), lambda b, h, *_: (0, 0))
    return pl.pallas_call(
        functools.partial(_dattn_kernel, n_plain),
        grid_spec=pltpu.PrefetchScalarGridSpec(
            num_scalar_prefetch=1,
            grid=(B, DIFF_HEADS),
            in_specs=[
                pl.BlockSpec((1, 1) + qd.shape[2:], lambda b, h, *_: (b, h, 0, 0, 0)),
                pl.BlockSpec((1, 1, S, LANES), lambda b, h, *_: (b, h, 0, 0)),
                pl.BlockSpec((1, 1) + vd.shape[2:], lambda b, h, *_: (b, h, 0, 0, 0)),
                pl.BlockSpec(masks.shape, lambda b, h, *_: (0, 0, 0)),
                vec(), vec(), vec(), vec(),
                pl.BlockSpec((DIFF_VD, ATT_TQ), lambda b, h, *_: (0, 0)),
            ],
            out_specs=pl.BlockSpec((1, S, DIFF_VD), lambda b, h, *_: (b, 0, h)),
            scratch_shapes=_sweep_scratch(2, nq, DIFF_VD),
        ),
        out_shape=jax.ShapeDtypeStruct((B, S, DIFF_HEADS * DIFF_VD), jnp.bfloat16),
        compiler_params=pltpu.CompilerParams(
            dimension_semantics=("arbitrary", "arbitrary"), vmem_limit_bytes=VMEM_LIMIT_BYTES),
        name="dattn",
    )(tab, qd, kd, vd, masks, lq1, lk1, lq2, lk2, gcol)


def _fattn_kernel(n_plain, tab_ref, q_ref, k_ref, v_ref, mask_ref, o_ref, m_s, acc_s, sbuf, pbuf, x0, x1, a0, a1):
    nq = m_s.shape[1]
    xbuf, abuf = (x0, x1), (a0, a1)

    def scores(h, it, jt):
        q = q_ref[0, h, it]
        return jnp.dot(k_ref[0, h, _key_slice(jt), :], q, preferred_element_type=jnp.float32)

    def values(h, jt):
        return v_ref[0, h, jt]

    _init_sweep_state(m_s, acc_s)
    _flat_sweep(tab_ref, mask_ref, n_plain, 2, scores, values, m_s, acc_s, sbuf, xbuf, pbuf, abuf)

    def finalize(it, c):
        heads = [acc_s[h, it] for h in range(2)]
        o = jnp.concatenate([a[:FOX_DH] / a[FOX_DH:FOX_DH + 1] for a in heads], axis=0)
        o_ref[0, _query_slice(it), :] = o.T.astype(o_ref.dtype)
        return c

    _for_each_query_block(nq, finalize)


def _fattn_call(tab, n_plain, masks, qf, kf, vf):
    B, _, nq = qf.shape[:3]
    S = nq * ATT_TQ
    return pl.pallas_call(
        functools.partial(_fattn_kernel, n_plain),
        grid_spec=pltpu.PrefetchScalarGridSpec(
            num_scalar_prefetch=1,
            grid=(B, FOX_HEADS // 2),
            in_specs=[
                pl.BlockSpec((1, 2) + qf.shape[2:], lambda b, p, *_: (b, p, 0, 0, 0)),
                pl.BlockSpec((1, 2, S, LANES), lambda b, p, *_: (b, p, 0, 0)),
                pl.BlockSpec((1, 2) + vf.shape[2:], lambda b, p, *_: (b, p, 0, 0, 0)),
                pl.BlockSpec(masks.shape, lambda b, p, *_: (0, 0, 0)),
            ],
            out_specs=pl.BlockSpec((1, S, 2 * FOX_DH), lambda b, p, *_: (b, 0, p)),
            scratch_shapes=_sweep_scratch(2, nq, FOX_DH),
        ),
        out_shape=jax.ShapeDtypeStruct((B, S, FOX_HEADS * FOX_DH), jnp.bfloat16),
        compiler_params=pltpu.CompilerParams(
            dimension_semantics=("arbitrary", "arbitrary"), vmem_limit_bytes=VMEM_LIMIT_BYTES),
        name="fattn",
    )(tab, qf, kf, vf, masks)


def _ffn_kernel(x_ref, md_ref, mf_ref, wod_ref, wof_ref, g2_ref, wg_ref, wu_ref, wd_ref, gf_ref, o_ref):
    x1 = (x_ref[...]
          + jnp.dot(md_ref[...], wod_ref[...], preferred_element_type=jnp.float32)
          + jnp.dot(mf_ref[...], wof_ref[...], preferred_element_type=jnp.float32))
    ms = jnp.mean(x1 * x1, axis=-1, keepdims=True)
    h2 = (x1 * lax.rsqrt(ms + EPS) * g2_ref[...]).astype(jnp.bfloat16)
    g = jnp.dot(h2, wg_ref[...], preferred_element_type=jnp.float32)
    u = jnp.dot(h2, wu_ref[...], preferred_element_type=jnp.float32)
    act = (g * jax.nn.sigmoid(g) * u).astype(jnp.bfloat16)
    y = x1 + jnp.dot(act, wd_ref[...], preferred_element_type=jnp.float32)
    ms = jnp.mean(y * y, axis=-1, keepdims=True)
    o_ref[...] = y * lax.rsqrt(ms + EPS) * gf_ref[...]


def _ffn_call(x2, md, mf, wod, wof, g2, wg, wu, wd, gf):
    n, D = x2.shape
    tm = FFN_TM
    half = md.shape[1]
    resident = lambda shape: pl.BlockSpec(shape, lambda i: (0, 0), pipeline_mode=pl.Buffered(1))
    return pl.pallas_call(
        _ffn_kernel,
        grid=(n // tm,),
        in_specs=[
            pl.BlockSpec((tm, D), lambda i: (i, 0)),
            pl.BlockSpec((tm, half), lambda i: (i, 0)),
            pl.BlockSpec((tm, half), lambda i: (i, 0)),
            resident((half, D)),
            resident((half, D)),
            resident((1, D)),
            resident((D, D_FF)),
            resident((D, D_FF)),
            resident((D_FF, D)),
            resident((1, D)),
        ],
        out_specs=pl.BlockSpec((tm, D), lambda i: (i, 0)),
        out_shape=jax.ShapeDtypeStruct((n, D), jnp.float32),
        compiler_params=pltpu.CompilerParams(
            dimension_semantics=("arbitrary",), vmem_limit_bytes=VMEM_LIMIT_BYTES),
        name="ffn",
    )(x2, md, mf, wod, wof, g2, wg, wu, wd, gf)


def _placement_tables():
    en = np.zeros((LANES, KF_W), np.float32)
    pt = np.zeros((FOX_HEADS * AUG_ROWS, 4 * TR_FG), np.float32)
    onesn = np.zeros((1, KF_W), np.float32)
    for h in range(FOX_HEADS):
        for p in range(3):
            en[p * FOX_HEADS + h, h * LANES + AUG_BIAS + p] = 1.0
            onesn[0, h * LANES + AUG_ONES + p] = 1.0
            pt[h * AUG_ROWS + (AUG_ONES - FOX_DH) + p, p * TR_FG + h] = 1.0
            pt[h * AUG_ROWS + (AUG_BIAS - FOX_DH) + p, 3 * TR_FG] = 1.0
    tri = np.tril(np.ones((PROJ_TM, PROJ_TM), np.float32))
    return en, pt, onesn, tri


def _rope_tables(S):
    f32 = jnp.float32
    pos = jnp.arange(S, dtype=jnp.int32)
    inv_freq = ROPE_THETA ** (-jnp.arange(0, ROPE_DIM, 2, dtype=f32) / ROPE_DIM)
    ang = pos.astype(f32)[:, None] * inv_freq[None, :]
    cos, sin = jnp.cos(ang), jnp.sin(ang)
    lane = np.arange(LANES) % DIFF_DH
    sel = lane % ROPE_HALF
    lo = jnp.asarray(lane < ROPE_HALF)[None, :]
    hi = jnp.asarray((lane >= ROPE_HALF) & (lane < ROPE_DIM))[None, :]
    cos_l, sin_l = cos[:, sel], sin[:, sel]
    cn = jnp.where(lo | hi, cos_l, 1.0)
    s1n = jnp.where(lo, -sin_l, 0.0)
    s2n = jnp.where(hi, sin_l, 0.0)
    return cn, s1n, s2n, cos.T, sin.T


def kernel(x, norm1_g, w_in, b_f, lam_q1, lam_k1, lam_q2, lam_k2, subln_g, w_out, norm2_g, w_gate, w_up,
           w_down, normf_g):
    f32, bf16 = jnp.float32, jnp.bfloat16
    B, S, D = x.shape
    en, pt, onesn, tri = _placement_tables()
    cn, s1n, s2n, ct, st = _rope_tables(S)

    w = w_in[0]
    o_dq, o_dk, o_dv, o_fq, o_fk, o_fv, o_fg = 0, 512, 1024, 1536, 2048, 2560, 3072
    wnat = jnp.concatenate([w[:, o_dk:o_dv], w[:, o_fk:o_fv]], axis=1).astype(bf16)
    wtr = jnp.concatenate(
        [w[:, o_dq:o_dk], w[:, o_dv:o_fq], w[:, o_fq:o_fk], w[:, o_fv:o_fg],
         jnp.pad(w[:, o_fg:], ((0, 0), (0, TR_FG - FOX_HEADS)))], axis=1).T.astype(bf16)
    bft = jnp.broadcast_to(jnp.pad(b_f[0].astype(f32), (0, TR_FG - FOX_HEADS))[:, None], (TR_FG, PROJ_TM))

    kd, kf, qd, vd, qf, vf = _proj_call(
        x, norm1_g[0][None, :], wnat, wtr, bft, cn, s1n, s2n, ct, st,
        jnp.asarray(tri, bf16), jnp.asarray(en, bf16), jnp.asarray(pt, bf16), jnp.asarray(onesn, f32))

    gcol = jnp.broadcast_to(subln_g[0].astype(f32)[:, None], (DIFF_VD, ATT_TQ))
    row = lambda v: v[0].astype(f32)[None, :]
    tab, n_plain, masks_d = _sweep_tables(S, CHUNK)
    _, _, masks_f = _sweep_tables(S, 1)
    tab = jnp.asarray(tab)
    mixd = _dattn_call(tab, n_plain, jnp.asarray(masks_d), qd, kd, vd,
                       row(lam_q1), row(lam_k1), row(lam_q2), row(lam_k2), gcol)
    mixf = _fattn_call(tab, n_plain, jnp.asarray(masks_f), qf, kf, vf)

    wo = w_out[0].astype(bf16)
    half = DIFF_HEADS * DIFF_VD
    out = _ffn_call(
        x.reshape(B * S, D), mixd.reshape(B * S, half), mixf.reshape(B * S, half),
        wo[:half], wo[half:], norm2_g[0][None, :], w_gate[0].astype(bf16), w_up[0].astype(bf16),
        w_down[0].astype(bf16), normf_g[None, :])
    return out.reshape(B, S, D)
```

```python
import functools
import math

import jax
import jax.numpy as jnp
import numpy as np
from jax import lax
from jax.experimental import pallas as pl
from jax.experimental.pallas import tpu as pltpu

D_MODEL = 1024
CHUNK = 64
DIFF_HEADS = 4
DIFF_DH = 64
DIFF_VD = 128
FOX_HEADS = 8
FOX_DH = 64
ROPE_THETA = 500000.0
ROPE_DIM = 16
ROPE_HALF = 8
D_FF = 2816
EPS = 1e-5
NEG = -1e30
LAM_INIT = 0.8 - 0.6 * math.exp(-0.3 * 0)
LOG2E = 1.4426950408889634

LANES = 128
MXU_N = 256
VMEM_LIMIT_BYTES = 56 * 1024 * 1024

PROJ_TM = 1024
ATT_TQ = MXU_N
ATT_TK = 512
ATT_GROUP = 2
ATT_UNROLL = 2
ATT_SLOTS = 2
ATT_FIN_UNROLL = 4
ATT_ONES_ROWS = 16
FFN_TM = 512

NAT_DK = DIFF_HEADS * 2 * DIFF_DH
NAT_FK = FOX_HEADS * FOX_DH
NAT_W = NAT_DK + NAT_FK
TR_DQ = DIFF_HEADS * 2 * DIFF_DH
TR_DV = DIFF_HEADS * DIFF_VD
TR_FQ = FOX_HEADS * FOX_DH
TR_FV = FOX_HEADS * FOX_DH
TR_FG = 16
TR_W = TR_DQ + TR_DV + TR_FQ + TR_FV + TR_FG
AUG_ONES = FOX_DH
AUG_BIAS = FOX_DH + 3
AUG_ROWS = 16
KF_W = FOX_HEADS * LANES


def _nt_dot(a, b):
    return lax.dot_general(a, b, (((1,), (1,)), ((), ())), preferred_element_type=jnp.float32)


def _split3(x):
    hi = x.astype(jnp.bfloat16)
    r = x - hi.astype(jnp.float32)
    mid = r.astype(jnp.bfloat16)
    lo = (r - mid.astype(jnp.float32)).astype(jnp.bfloat16)
    return hi, mid, lo


def _log_sigmoid(x):
    return jnp.minimum(x, 0.0) - jnp.log1p(jnp.exp(-jnp.abs(x)))


def _store_tiles(ref, x, rows, cols):
    for hd in range(x.shape[0] // rows):
        for c in range(x.shape[1] // cols):
            ref[0, hd, c] = x[hd * rows:(hd + 1) * rows, c * cols:(c + 1) * cols]


def _proj_kernel(x_ref, g_ref, wnat_ref, wtr_ref, bft_ref, cn_ref, s1n_ref, s2n_ref, ct_ref, st_ref,
                 tri_ref, en_ref, pt_ref, onesn_ref,
                 kd_ref, kf_ref, qd_ref, vd_ref, qf_ref, vf_ref, carry_t):
    @pl.when(pl.program_id(1) == 0)
    def _():
        carry_t[...] = jnp.zeros_like(carry_t)

    f32, bf16 = jnp.float32, jnp.bfloat16
    tm = x_ref.shape[1]
    x = x_ref[0]
    ms = jnp.mean(x * x, axis=-1, keepdims=True)
    h = (x * lax.rsqrt(ms + EPS) * g_ref[...]).astype(bf16)

    nat = jnp.dot(h, wnat_ref[...], preferred_element_type=f32)
    tr = _nt_dot(wtr_ref[...], h)

    cn, s1n, s2n = cn_ref[...], s1n_ref[...], s2n_ref[...]
    for hd in range(DIFF_HEADS):
        t = nat[:, hd * LANES:(hd + 1) * LANES]
        rot = t * cn + pltpu.roll(t, LANES - ROPE_HALF, 1) * s1n + pltpu.roll(t, ROPE_HALF, 1) * s2n
        kd_ref[0, hd] = rot.astype(bf16)

    ct, st = ct_ref[...], st_ref[...]
    pieces = []
    for comp in range(DIFF_HEADS * 2):
        base = comp * DIFF_DH
        t1 = tr[base:base + ROPE_HALF]
        t2 = tr[base + ROPE_HALF:base + ROPE_DIM]
        pieces += [t1 * ct - t2 * st, t2 * ct + t1 * st, tr[base + ROPE_DIM:base + DIFF_DH]]
    qd = jnp.concatenate(pieces, axis=0) * (DIFF_DH ** -0.5 * LOG2E)
    _store_tiles(qd_ref, qd.astype(bf16), 2 * DIFF_DH, ATT_TQ)
    _store_tiles(vd_ref, tr[TR_DQ:TR_DQ + TR_DV].astype(bf16), DIFF_VD, ATT_TK)

    o_fq = TR_DQ + TR_DV
    o_fv = o_fq + TR_FQ
    o_fg = o_fv + TR_FV
    lf_t = _log_sigmoid(tr[o_fg:o_fg + TR_FG] + bft_ref[...])
    c3t = _nt_dot(jnp.concatenate(_split3(lf_t), axis=0), tri_ref[...])
    cum_t = (c3t[2 * TR_FG:] + c3t[TR_FG:2 * TR_FG]) + c3t[:TR_FG] + carry_t[:, 0:1]
    carry_t[...] = jnp.broadcast_to(cum_t[:, tm - 1:tm], carry_t.shape)

    qb = jnp.concatenate(_split3(cum_t * LOG2E) + (jnp.ones((TR_FG, tm), bf16),), axis=0)
    aug_q = jnp.dot(pt_ref[...], qb, preferred_element_type=f32)
    fq = tr[o_fq:o_fq + TR_FQ] * (FOX_DH ** -0.5 * LOG2E)
    pad_rows = jnp.zeros((LANES - FOX_DH - AUG_ROWS, tm), f32)
    for hd in range(FOX_HEADS):
        blk = jnp.concatenate([fq[hd * FOX_DH:(hd + 1) * FOX_DH], aug_q[hd * AUG_ROWS:(hd + 1) * AUG_ROWS],
                               pad_rows], axis=0).astype(bf16)
        for c in range(tm // ATT_TQ):
            qf_ref[0, hd, c] = blk[:, c * ATT_TQ:(c + 1) * ATT_TQ]
    _store_tiles(vf_ref, tr[o_fv:o_fv + TR_FV].astype(bf16), FOX_DH, ATT_TK)

    parts = [part.astype(f32)[:FOX_HEADS] for part in _split3(cum_t * (-LOG2E))]
    parts.append(jnp.zeros((LANES - 3 * FOX_HEADS, tm), f32))
    kb = jnp.concatenate(parts, axis=0).T.astype(bf16)
    aug_k = jnp.dot(kb, en_ref[...], preferred_element_type=f32) + onesn_ref[...]
    lane = lax.broadcasted_iota(jnp.int32, (tm, LANES), 1)
    for hd in range(FOX_HEADS):
        src = nat[:, NAT_DK + (hd // 2) * LANES:NAT_DK + (hd // 2 + 1) * LANES]
        if hd % 2:
            src = pltpu.roll(src, FOX_DH, 1)
        kf_ref[0, hd] = jnp.where(lane < FOX_DH, src, aug_k[:, hd * LANES:(hd + 1) * LANES]).astype(bf16)


def _proj_call(x, g1, wnat, wtr, bft, cn, s1n, s2n, ct, st, tri, en, pt, onesn):
    B, S, D = x.shape
    tm = PROJ_TM
    const = lambda shape: pl.BlockSpec(shape, lambda b, i: (0,) * len(shape))
    tiles = lambda heads, rows, cols: pl.BlockSpec((1, heads, tm // cols, rows, cols), lambda b, i: (b, 0, i, 0, 0))
    bf16 = jnp.bfloat16
    return pl.pallas_call(
        _proj_kernel,
        grid=(B, S // tm),
        in_specs=[
            pl.BlockSpec((1, tm, D), lambda b, i: (b, i, 0)),
            const((1, D)),
            const((D, NAT_W)),
            const((TR_W, D)),
            const((TR_FG, tm)),
            pl.BlockSpec((tm, LANES), lambda b, i: (i, 0)),
            pl.BlockSpec((tm, LANES), lambda b, i: (i, 0)),
            pl.BlockSpec((tm, LANES), lambda b, i: (i, 0)),
            pl.BlockSpec((ROPE_HALF, tm), lambda b, i: (0, i)),
            pl.BlockSpec((ROPE_HALF, tm), lambda b, i: (0, i)),
            const((tm, tm)),
            const((LANES, KF_W)),
            const((FOX_HEADS * AUG_ROWS, 4 * TR_FG)),
            const((1, KF_W)),
        ],
        out_specs=[
            pl.BlockSpec((1, DIFF_HEADS, tm, LANES), lambda b, i: (b, 0, i, 0)),
            pl.BlockSpec((1, FOX_HEADS, tm, LANES), lambda b, i: (b, 0, i, 0)),
            tiles(DIFF_HEADS, 2 * DIFF_DH, ATT_TQ),
            tiles(DIFF_HEADS, DIFF_VD, ATT_TK),
            tiles(FOX_HEADS, LANES, ATT_TQ),
            tiles(FOX_HEADS, FOX_DH, ATT_TK),
        ],
        out_shape=[
            jax.ShapeDtypeStruct((B, DIFF_HEADS, S, LANES), bf16),
            jax.ShapeDtypeStruct((B, FOX_HEADS, S, LANES), bf16),
            jax.ShapeDtypeStruct((B, DIFF_HEADS, S // ATT_TQ, 2 * DIFF_DH, ATT_TQ), bf16),
            jax.ShapeDtypeStruct((B, DIFF_HEADS, S // ATT_TK, DIFF_VD, ATT_TK), bf16),
            jax.ShapeDtypeStruct((B, FOX_HEADS, S // ATT_TQ, LANES, ATT_TQ), bf16),
            jax.ShapeDtypeStruct((B, FOX_HEADS, S // ATT_TK, FOX_DH, ATT_TK), bf16),
        ],
        scratch_shapes=[pltpu.VMEM((TR_FG, LANES), jnp.float32)],
        compiler_params=pltpu.CompilerParams(
            dimension_semantics=("arbitrary", "arbitrary"), vmem_limit_bytes=VMEM_LIMIT_BYTES),
        name="proj",
    )(x, g1, wnat, wtr, bft, cn, s1n, s2n, ct, st, tri, en, pt, onesn)


def _sweep_tables(S, chunk):
    nq, ratio = S // ATT_TQ, ATT_TK // ATT_TQ
    diag = [(i, i // ratio, i % ratio) for i in range(nq)]
    full = [(i, j, 0) for i in range(nq) for j in range(i // ratio)]
    assert len(full) % ATT_GROUP == 0 and len(diag) % ATT_GROUP == 0
    kk = np.arange(ATT_TK)[:, None] // chunk
    masks = []
    for r in range(ratio):
        qq = (r * ATT_TQ + np.arange(ATT_TQ))[None, :] // chunk
        masks.append(np.where(kk <= qq, 0.0, NEG).astype(np.float32))
    return np.array(full + diag, np.int32).T.copy(), len(full), np.stack(masks)


def _init_sweep_state(m_s, acc_s):
    m_s[...] = jnp.full_like(m_s, NEG)
    acc_s[...] = jnp.zeros_like(acc_s)


def _with_ones_rows(v_t):
    return jnp.concatenate([v_t, jnp.ones((ATT_ONES_ROWS, v_t.shape[1]), v_t.dtype)], axis=0)


def _flat_sweep(tab_ref, mask_ref, n_plain, n_heads, scores, values, m_s, acc_s, sbuf, xbuf, pbuf, abuf):
    n_groups, n_plain_groups = tab_ref.shape[1] // ATT_GROUP, n_plain // ATT_GROUP
    assert 2 <= n_plain_groups <= n_groups and ATT_UNROLL % ATT_SLOTS == 0

    def slot(par, a):
        return par * ATT_GROUP + a

    def stage_scores(u, par, masked):
        for a in range(ATT_GROUP):
            t = u * ATT_GROUP + a
            it, jt = tab_ref[0, t], tab_ref[1, t]
            for h in range(n_heads):
                s = scores(h, it, jt)
                if masked:
                    s = s + mask_ref[tab_ref[2, t]]
                sbuf[slot(par, a), h] = s
                xbuf[par][a, h] = jnp.max(s, axis=0, keepdims=True)

    def stage_softmax(u, par):
        for a in range(ATT_GROUP):
            it = tab_ref[0, u * ATT_GROUP + a]
            for h in range(n_heads):
                m_old = m_s[h, it]
                m_new = jnp.maximum(m_old, xbuf[par][a, h])
                alpha = jnp.exp2(m_old - m_new)
                p = jnp.exp2(sbuf[slot(par, a), h] - m_new)
                m_s[h, it] = m_new
                pbuf[slot(par, a), h] = p.astype(jnp.bfloat16)
                abuf[par][a, h] = alpha

    def stage_values(u, par):
        for a in range(ATT_GROUP):
            t = u * ATT_GROUP + a
            it, jt = tab_ref[0, t], tab_ref[1, t]
            for h in range(n_heads):
                pv = jnp.dot(_with_ones_rows(values(h, jt)), pbuf[slot(par, a), h],
                             preferred_element_type=jnp.float32)
                acc_s[h, it] = abuf[par][a, h] * acc_s[h, it] + pv

    def step(u, par, masked, with_values=True):
        stage_scores(u, par, masked)
        if with_values:
            stage_values(u - 2, (par - 2) % ATT_SLOTS)
        stage_softmax(u - 1, (par - 1) % ATT_SLOTS)

    def run(lo, hi, masked):
        first = lo + (hi - lo) % ATT_UNROLL
        for u in range(lo, first):
            step(u, u % ATT_SLOTS, masked)

        def body(v, c):
            for k in range(ATT_UNROLL):
                step(first + ATT_UNROLL * v + k, (first + k) % ATT_SLOTS, masked)
            return c

        lax.fori_loop(0, (hi - first) // ATT_UNROLL, body, 0)

    stage_scores(0, 0, False)
    step(1, 1, False, with_values=False)
    run(2, n_plain_groups, False)
    run(n_plain_groups, n_groups, True)
    stage_values(n_groups - 2, (n_groups - 2) % ATT_SLOTS)
    stage_softmax(n_groups - 1, (n_groups - 1) % ATT_SLOTS)
    stage_values(n_groups - 1, (n_groups - 1) % ATT_SLOTS)


def _sweep_scratch(n_heads, nq, dv):
    f32 = jnp.float32
    slots = ATT_SLOTS * ATT_GROUP
    return [
        pltpu.VMEM((n_heads, nq, 1, ATT_TQ), f32),
        pltpu.VMEM((n_heads, nq, dv + ATT_ONES_ROWS, ATT_TQ), f32),
        pltpu.VMEM((slots, n_heads, ATT_TK, ATT_TQ), f32),
        pltpu.VMEM((slots, n_heads, ATT_TK, ATT_TQ), jnp.bfloat16),
    ] + [pltpu.VMEM((ATT_GROUP, n_heads, 1, ATT_TQ), f32)] * (2 * ATT_SLOTS)


def _for_each_query_block(nq, fn):
    def body(g, c):
        for k in range(ATT_FIN_UNROLL):
            fn(g * ATT_FIN_UNROLL + k, c)
        return c

    lax.fori_loop(0, nq // ATT_FIN_UNROLL, body, 0)


def _key_slice(jt):
    return pl.ds(pl.multiple_of(jt * ATT_TK, ATT_TK), ATT_TK)


def _query_slice(it):
    return pl.ds(pl.multiple_of(it * ATT_TQ, ATT_TQ), ATT_TQ)


def _dattn_kernel(n_plain, tab_ref, q_ref, k_ref, v_ref, mask_ref, lq1_ref, lk1_ref, lq2_ref, lk2_ref, g_ref,
                  o_ref, m_s, acc_s, sbuf, pbuf, *rowbufs):
    nq = m_s.shape[1]
    xbuf, abuf = rowbufs[:ATT_SLOTS], rowbufs[ATT_SLOTS:]
    row = lax.broadcasted_iota(jnp.int32, (2 * DIFF_DH, ATT_TQ), 0)

    def scores(h, it, jt):
        q = q_ref[0, 0, it]
        keep = (row < DIFF_DH) if h == 0 else (row >= DIFF_DH)
        rhs = jnp.where(keep, q, jnp.zeros_like(q))
        return jnp.dot(k_ref[0, 0, _key_slice(jt), :], rhs, preferred_element_type=jnp.float32)

    def values(h, jt):
        return v_ref[0, 0, jt]

    _init_sweep_state(m_s, acc_s)
    _flat_sweep(tab_ref, mask_ref, n_plain, 2, scores, values, m_s, acc_s, sbuf, xbuf, pbuf, abuf)

    lam = (jnp.exp(jnp.sum(lq1_ref[...] * lk1_ref[...], axis=-1, keepdims=True))
           - jnp.exp(jnp.sum(lq2_ref[...] * lk2_ref[...], axis=-1, keepdims=True)) + LAM_INIT)
    gain = g_ref[...] * (1.0 - LAM_INIT)

    def finalize(it, c):
        a1, a2 = acc_s[0, it], acc_s[1, it]
        o = (a1[:DIFF_VD] / a1[DIFF_VD:DIFF_VD + 1]
             - lam * (a2[:DIFF_VD] / a2[DIFF_VD:DIFF_VD + 1]))
        ms = jnp.mean(o * o, axis=0, keepdims=True)
        o = o * lax.rsqrt(ms + EPS) * gain
        o_ref[0, _query_slice(it), :] = o.T.astype(o_ref.dtype)
        return c

    _for_each_query_block(nq, finalize)


def _dattn_call(tab, n_plain, masks, qd, kd, vd, lq1, lk1, lq2, lk2, gcol):
    B, _, nq = qd.shape[:3]
    S = nq * ATT_TQ
    vec = lambda: pl.BlockSpec((1, DIFF_DH), lambda b, h, *_: (0, 0))
    return pl.pallas_call(
        functools.partial(_dattn_kernel, n_plain),
        grid_spec=pltpu.PrefetchScalarGridSpec(
            num_scalar_prefetch=1,
            grid=(B, DIFF_HEADS),
            in_specs=[
                pl.BlockSpec((1, 1) + qd.shape[2:], lambda b, h, *_: (b, h, 0, 0, 0)),
                pl.BlockSpec((1, 1, S, LANES), lambda b, h, *_: (b, h, 0, 0)),
                pl.BlockSpec((1, 1) + vd.shape[2:], lambda b, h, *_: (b, h, 0, 0, 0)),
                pl.BlockSpec(masks.shape, lambda b, h, *_: (0, 0, 0)),
                vec(), vec(), vec(), vec(),
                pl.BlockSpec((DIFF_VD, ATT_TQ), lambda b, h, *_: (0, 0)),
            ],
            out_specs=pl.BlockSpec((1, S, DIFF_VD), lambda b, h, *_: (b, 0, h)),
            scratch_shapes=_sweep_scratch(2, nq, DIFF_VD),
        ),
        out_shape=jax.ShapeDtypeStruct((B, S, DIFF_HEADS * DIFF_VD), jnp.bfloat16),
        compiler_params=pltpu.CompilerParams(
            dimension_semantics=("arbitrary", "arbitrary"), vmem_limit_bytes=VMEM_LIMIT_BYTES),
        name="dattn",
    )(tab, qd, kd, vd, masks, lq1, lk1, lq2, lk2, gcol)


def _fattn_kernel(n_plain, tab_ref, q_ref, k_ref, v_ref, mask_ref, o_ref, m_s, acc_s, sbuf, pbuf, *rowbufs):
    nq = m_s.shape[1]
    xbuf, abuf = rowbufs[:ATT_SLOTS], rowbufs[ATT_SLOTS:]

    def scores(h, it, jt):
        q = q_ref[0, h, it]
        return jnp.dot(k_ref[0, h, _key_slice(jt), :], q, preferred_element_type=jnp.float32)

    def values(h, jt):
        return v_ref[0, h, jt]

    _init_sweep_state(m_s, acc_s)
    _flat_sweep(tab_ref, mask_ref, n_plain, 2, scores, values, m_s, acc_s, sbuf, xbuf, pbuf, abuf)

    def finalize(it, c):
        heads = [acc_s[h, it] for h in range(2)]
        o = jnp.concatenate([a[:FOX_DH] / a[FOX_DH:FOX_DH + 1] for a in heads], axis=0)
        o_ref[0, _query_slice(it), :] = o.T.astype(o_ref.dtype)
        return c

    _for_each_query_block(nq, finalize)


def _fattn_call(tab, n_plain, masks, qf, kf, vf):
    B, _, nq = qf.shape[:3]
    S = nq * ATT_TQ
    return pl.pallas_call(
        functools.partial(_fattn_kernel, n_plain),
        grid_spec=pltpu.PrefetchScalarGridSpec(
            num_scalar_prefetch=1,
            grid=(B, FOX_HEADS // 2),
            in_specs=[
                pl.BlockSpec((1, 2) + qf.shape[2:], lambda b, p, *_: (b, p, 0, 0, 0)),
                pl.BlockSpec((1, 2, S, LANES), lambda b, p, *_: (b, p, 0, 0)),
                pl.BlockSpec((1, 2) + vf.shape[2:], lambda b, p, *_: (b, p, 0, 0, 0)),
                pl.BlockSpec(masks.shape, lambda b, p, *_: (0, 0, 0)),
            ],
            out_specs=pl.BlockSpec((1, S, 2 * FOX_DH), lambda b, p, *_: (b, 0, p)),
            scratch_shapes=_sweep_scratch(2, nq, FOX_DH),
        ),
        out_shape=jax.ShapeDtypeStruct((B, S, FOX_HEADS * FOX_DH), jnp.bfloat16),
        compiler_params=pltpu.CompilerParams(
            dimension_semantics=("arbitrary", "arbitrary"), vmem_limit_bytes=VMEM_LIMIT_BYTES),
        name="fattn",
    )(tab, qf, kf, vf, masks)


def _ffn_kernel(x_ref, md_ref, mf_ref, wo_ref, g2_ref, wg_ref, wu_ref, wd_ref, gf_ref, o_ref):
    half = md_ref.shape[1]
    x1 = (x_ref[...]
          + jnp.dot(md_ref[...], wo_ref[:half], preferred_element_type=jnp.float32)
          + jnp.dot(mf_ref[...], wo_ref[half:], preferred_element_type=jnp.float32))
    ms = jnp.mean(x1 * x1, axis=-1, keepdims=True)
    h2 = (x1 * lax.rsqrt(ms + EPS) * g2_ref[...]).astype(jnp.bfloat16)
    g = jnp.dot(h2, wg_ref[...], preferred_element_type=jnp.float32)
    u = jnp.dot(h2, wu_ref[...], preferred_element_type=jnp.float32)
    act = (g * jax.nn.sigmoid(g) * u).astype(jnp.bfloat16)
    y = x1 + jnp.dot(act, wd_ref[...], preferred_element_type=jnp.float32)
    ms = jnp.mean(y * y, axis=-1, keepdims=True)
    o_ref[...] = y * lax.rsqrt(ms + EPS) * gf_ref[...]


def _ffn_call(x2, md, mf, wo, g2, wg, wu, wd, gf):
    n, D = x2.shape
    tm = FFN_TM
    half = md.shape[1]
    resident = lambda shape: pl.BlockSpec(shape, lambda i: (0, 0), pipeline_mode=pl.Buffered(1))
    return pl.pallas_call(
        _ffn_kernel,
        grid=(n // tm,),
        in_specs=[
            pl.BlockSpec((tm, D), lambda i: (i, 0)),
            pl.BlockSpec((tm, half), lambda i: (i, 0)),
            pl.BlockSpec((tm, half), lambda i: (i, 0)),
            resident((2 * half, D)),
            resident((1, D)),
            resident((D, D_FF)),
            resident((D, D_FF)),
            resident((D_FF, D)),
            resident((1, D)),
        ],
        out_specs=pl.BlockSpec((tm, D), lambda i: (i, 0)),
        out_shape=jax.ShapeDtypeStruct((n, D), jnp.float32),
        compiler_params=pltpu.CompilerParams(
            dimension_semantics=("arbitrary",), vmem_limit_bytes=VMEM_LIMIT_BYTES),
        name="ffn",
    )(x2, md, mf, wo, g2, wg, wu, wd, gf)


def _placement_tables():
    en = np.zeros((LANES, KF_W), np.float32)
    pt = np.zeros((FOX_HEADS * AUG_ROWS, 4 * TR_FG), np.float32)
    onesn = np.zeros((1, KF_W), np.float32)
    for h in range(FOX_HEADS):
        for p in range(3):
            en[p * FOX_HEADS + h, h * LANES + AUG_BIAS + p] = 1.0
            onesn[0, h * LANES + AUG_ONES + p] = 1.0
            pt[h * AUG_ROWS + (AUG_ONES - FOX_DH) + p, p * TR_FG + h] = 1.0
            pt[h * AUG_ROWS + (AUG_BIAS - FOX_DH) + p, 3 * TR_FG] = 1.0
    tri = np.tril(np.ones((PROJ_TM, PROJ_TM), np.float32))
    return en, pt, onesn, tri


def _rope_tables(S):
    inv_freq = ROPE_THETA ** (-np.arange(0, ROPE_DIM, 2, dtype=np.float64) / ROPE_DIM)
    ang = np.arange(S, dtype=np.float64)[:, None] * inv_freq[None, :]
    cos, sin = np.cos(ang), np.sin(ang)
    lane = np.arange(LANES) % DIFF_DH
    sel = lane % ROPE_HALF
    lo = (lane < ROPE_HALF)[None, :]
    hi = ((lane >= ROPE_HALF) & (lane < ROPE_DIM))[None, :]
    cos_l, sin_l = cos[:, sel], sin[:, sel]
    cn = np.where(lo | hi, cos_l, 1.0)
    s1n = np.where(lo, -sin_l, 0.0)
    s2n = np.where(hi, sin_l, 0.0)
    return [jnp.asarray(t, jnp.float32) for t in (cn, s1n, s2n, cos.T, sin.T)]


def kernel(x, norm1_g, w_in, b_f, lam_q1, lam_k1, lam_q2, lam_k2, subln_g, w_out, norm2_g, w_gate, w_up,
           w_down, normf_g):
    f32, bf16 = jnp.float32, jnp.bfloat16
    B, S, D = x.shape
    en, pt, onesn, tri = _placement_tables()
    cn, s1n, s2n, ct, st = _rope_tables(S)

    w = w_in[0]
    o_dq, o_dk, o_dv, o_fq, o_fk, o_fv, o_fg = 0, 512, 1024, 1536, 2048, 2560, 3072
    wnat = jnp.concatenate([w[:, o_dk:o_dv], w[:, o_fk:o_fv]], axis=1).astype(bf16)
    wtr = jnp.concatenate(
        [w[:, o_dq:o_dk], w[:, o_dv:o_fq], w[:, o_fq:o_fk], w[:, o_fv:o_fg],
         jnp.pad(w[:, o_fg:], ((0, 0), (0, TR_FG - FOX_HEADS)))], axis=1).T.astype(bf16)
    bft = jnp.broadcast_to(jnp.pad(b_f[0].astype(f32), (0, TR_FG - FOX_HEADS))[:, None], (TR_FG, PROJ_TM))

    kd, kf, qd, vd, qf, vf = _proj_call(
        x, norm1_g[0][None, :], wnat, wtr, bft, cn, s1n, s2n, ct, st,
        jnp.asarray(tri, bf16), jnp.asarray(en, bf16), jnp.asarray(pt, bf16), jnp.asarray(onesn, f32))

    gcol = jnp.broadcast_to(subln_g[0].astype(f32)[:, None], (DIFF_VD, ATT_TQ))
    row = lambda v: v[0].astype(f32)[None, :]
    tab, n_plain, masks_d = _sweep_tables(S, CHUNK)
    _, _, masks_f = _sweep_tables(S, 1)
    tab = jnp.asarray(tab)
    mixd = _dattn_call(tab, n_plain, jnp.asarray(masks_d), qd, kd, vd,
                       row(lam_q1), row(lam_k1), row(lam_q2), row(lam_k2), gcol)
    mixf = _fattn_call(tab, n_plain, jnp.asarray(masks_f), qf, kf, vf)

    half = DIFF_HEADS * DIFF_VD
    out = _ffn_call(
        x.reshape(B * S, D), mixd.reshape(B * S, half), mixf.reshape(B * S, half),
        w_out[0].astype(bf16), norm2_g[0][None, :], w_gate[0].astype(bf16), w_up[0].astype(bf16),
        w_down[0].astype(bf16), normf_g[None, :])
    return out.reshape(B, S, D)
```

```python
import functools
import math

import jax
import jax.numpy as jnp
import numpy as np
from jax import lax
from jax.experimental import pallas as pl
from jax.experimental.pallas import tpu as pltpu

D_MODEL = 1024
CHUNK = 64
DIFF_HEADS = 4
DIFF_DH = 64
DIFF_VD = 128
FOX_HEADS = 8
FOX_DH = 64
ROPE_THETA = 500000.0
ROPE_DIM = 16
ROPE_HALF = 8
D_FF = 2816
EPS = 1e-5
NEG = -1e30
LAM_INIT = 0.8 - 0.6 * math.exp(-0.3 * 0)
LOG2E = 1.4426950408889634

LANES = 128
SUBLANES = 8
MXU_N = 256
VMEM_LIMIT_BYTES = 56 * 1024 * 1024

PROJ_TM = 1024
ATT_TQ = MXU_N
ATT_TK = 512
ATT_GROUP = 4
ATT_BURST = 2
ATT_UNROLL = 2
ATT_SLOTS = 2
ATT_FIN_UNROLL = 4
ATT_ONES_ROWS = 16
FFN_TM = 512

NAT_DK = DIFF_HEADS * 2 * DIFF_DH
NAT_FK = FOX_HEADS * FOX_DH
NAT_W = NAT_DK + NAT_FK
TR_DQ = DIFF_HEADS * 2 * DIFF_DH
TR_DV = DIFF_HEADS * DIFF_VD
TR_FQ = FOX_HEADS * FOX_DH
TR_FV = FOX_HEADS * FOX_DH
TR_FG = 16
TR_W = TR_DQ + TR_DV + TR_FQ + TR_FV + TR_FG
AUG_ONES = FOX_DH
AUG_BIAS = FOX_DH + 3
AUG_ROWS = 16
KF_W = FOX_HEADS * LANES


def _nt_dot(a, b):
    return lax.dot_general(a, b, (((1,), (1,)), ((), ())), preferred_element_type=jnp.float32)


def _split3(x):
    hi = x.astype(jnp.bfloat16)
    r = x - hi.astype(jnp.float32)
    mid = r.astype(jnp.bfloat16)
    lo = (r - mid.astype(jnp.float32)).astype(jnp.bfloat16)
    return hi, mid, lo


def _log_sigmoid(x):
    return jnp.minimum(x, 0.0) - jnp.log1p(jnp.exp(-jnp.abs(x)))


def _store_tiles(ref, x, rows, cols):
    for hd in range(x.shape[0] // rows):
        for c in range(x.shape[1] // cols):
            ref[0, hd, c] = x[hd * rows:(hd + 1) * rows, c * cols:(c + 1) * cols]


def _proj_kernel(x_ref, g_ref, wnat_ref, wtr_ref, bft_ref, cn_ref, s1n_ref, s2n_ref, ct_ref, st_ref,
                 tri_ref, en_ref, pt_ref, onesn_ref,
                 kd_ref, kf_ref, qd_ref, vd_ref, qf_ref, vf_ref, carry_t):
    @pl.when(pl.program_id(1) == 0)
    def _():
        carry_t[...] = jnp.zeros_like(carry_t)

    f32, bf16 = jnp.float32, jnp.bfloat16
    tm = x_ref.shape[1]
    x = x_ref[0]
    ms = jnp.mean(x * x, axis=-1, keepdims=True)
    h = (x * lax.rsqrt(ms + EPS) * g_ref[...]).astype(bf16)

    nat = jnp.dot(h, wnat_ref[...], preferred_element_type=f32)
    tr = _nt_dot(wtr_ref[...], h)

    cn, s1n, s2n = cn_ref[...], s1n_ref[...], s2n_ref[...]
    for hd in range(DIFF_HEADS):
        t = nat[:, hd * LANES:(hd + 1) * LANES]
        rot = t * cn + pltpu.roll(t, LANES - ROPE_HALF, 1) * s1n + pltpu.roll(t, ROPE_HALF, 1) * s2n
        kd_ref[0, hd] = rot.astype(bf16)

    ct, st = ct_ref[...], st_ref[...]
    pieces = []
    for comp in range(DIFF_HEADS * 2):
        base = comp * DIFF_DH
        t1 = tr[base:base + ROPE_HALF]
        t2 = tr[base + ROPE_HALF:base + ROPE_DIM]
        pieces += [t1 * ct - t2 * st, t2 * ct + t1 * st, tr[base + ROPE_DIM:base + DIFF_DH]]
    qd = jnp.concatenate(pieces, axis=0) * (DIFF_DH ** -0.5 * LOG2E)
    _store_tiles(qd_ref, qd.astype(bf16), 2 * DIFF_DH, ATT_TQ)
    _store_tiles(vd_ref, tr[TR_DQ:TR_DQ + TR_DV].astype(bf16), DIFF_VD, ATT_TK)

    o_fq = TR_DQ + TR_DV
    o_fv = o_fq + TR_FQ
    o_fg = o_fv + TR_FV
    lf_t = _log_sigmoid(tr[o_fg:o_fg + TR_FG] + bft_ref[...])
    c3t = _nt_dot(jnp.concatenate(_split3(lf_t), axis=0), tri_ref[...])
    cum_t = (c3t[2 * TR_FG:] + c3t[TR_FG:2 * TR_FG]) + c3t[:TR_FG] + carry_t[:, 0:1]
    carry_t[...] = jnp.broadcast_to(cum_t[:, tm - 1:tm], carry_t.shape)

    qb = jnp.concatenate(_split3(cum_t * LOG2E) + (jnp.ones((TR_FG, tm), bf16),), axis=0)
    aug_q = jnp.dot(pt_ref[...], qb, preferred_element_type=f32)
    fq = tr[o_fq:o_fq + TR_FQ] * (FOX_DH ** -0.5 * LOG2E)
    pad_rows = jnp.zeros((LANES - FOX_DH - AUG_ROWS, tm), f32)
    for hd in range(FOX_HEADS):
        blk = jnp.concatenate([fq[hd * FOX_DH:(hd + 1) * FOX_DH], aug_q[hd * AUG_ROWS:(hd + 1) * AUG_ROWS],
                               pad_rows], axis=0).astype(bf16)
        for c in range(tm // ATT_TQ):
            qf_ref[0, hd, c] = blk[:, c * ATT_TQ:(c + 1) * ATT_TQ]
    _store_tiles(vf_ref, tr[o_fv:o_fv + TR_FV].astype(bf16), FOX_DH, ATT_TK)

    parts = [part.astype(f32)[:FOX_HEADS] for part in _split3(cum_t * (-LOG2E))]
    parts.append(jnp.zeros((LANES - 3 * FOX_HEADS, tm), f32))
    kb = jnp.concatenate(parts, axis=0).T.astype(bf16)
    aug_k = jnp.dot(kb, en_ref[...], preferred_element_type=f32) + onesn_ref[...]
    lane = lax.broadcasted_iota(jnp.int32, (tm, LANES), 1)
    for hd in range(FOX_HEADS):
        src = nat[:, NAT_DK + (hd // 2) * LANES:NAT_DK + (hd // 2 + 1) * LANES]
        if hd % 2:
            src = pltpu.roll(src, FOX_DH, 1)
        kf_ref[0, hd] = jnp.where(lane < FOX_DH, src, aug_k[:, hd * LANES:(hd + 1) * LANES]).astype(bf16)


def _proj_call(x, g1, wnat, wtr, bft, cn, s1n, s2n, ct, st, tri, en, pt, onesn):
    B, S, D = x.shape
    tm = PROJ_TM
    const = lambda shape: pl.BlockSpec(shape, lambda b, i: (0,) * len(shape))
    tiles = lambda heads, rows, cols: pl.BlockSpec((1, heads, tm // cols, rows, cols), lambda b, i: (b, 0, i, 0, 0))
    bf16 = jnp.bfloat16
    return pl.pallas_call(
        _proj_kernel,
        grid=(B, S // tm),
        in_specs=[
            pl.BlockSpec((1, tm, D), lambda b, i: (b, i, 0)),
            const((1, D)),
            const((D, NAT_W)),
            const((TR_W, D)),
            const((TR_FG, tm)),
            pl.BlockSpec((tm, LANES), lambda b, i: (i, 0)),
            pl.BlockSpec((tm, LANES), lambda b, i: (i, 0)),
            pl.BlockSpec((tm, LANES), lambda b, i: (i, 0)),
            pl.BlockSpec((ROPE_HALF, tm), lambda b, i: (0, i)),
            pl.BlockSpec((ROPE_HALF, tm), lambda b, i: (0, i)),
            const((tm, tm)),
            const((LANES, KF_W)),
            const((FOX_HEADS * AUG_ROWS, 4 * TR_FG)),
            const((1, KF_W)),
        ],
        out_specs=[
            pl.BlockSpec((1, DIFF_HEADS, tm, LANES), lambda b, i: (b, 0, i, 0)),
            pl.BlockSpec((1, FOX_HEADS, tm, LANES), lambda b, i: (b, 0, i, 0)),
            tiles(DIFF_HEADS, 2 * DIFF_DH, ATT_TQ),
            tiles(DIFF_HEADS, DIFF_VD, ATT_TK),
            tiles(FOX_HEADS, LANES, ATT_TQ),
            tiles(FOX_HEADS, FOX_DH, ATT_TK),
        ],
        out_shape=[
            jax.ShapeDtypeStruct((B, DIFF_HEADS, S, LANES), bf16),
            jax.ShapeDtypeStruct((B, FOX_HEADS, S, LANES), bf16),
            jax.ShapeDtypeStruct((B, DIFF_HEADS, S // ATT_TQ, 2 * DIFF_DH, ATT_TQ), bf16),
            jax.ShapeDtypeStruct((B, DIFF_HEADS, S // ATT_TK, DIFF_VD, ATT_TK), bf16),
            jax.ShapeDtypeStruct((B, FOX_HEADS, S // ATT_TQ, LANES, ATT_TQ), bf16),
            jax.ShapeDtypeStruct((B, FOX_HEADS, S // ATT_TK, FOX_DH, ATT_TK), bf16),
        ],
        scratch_shapes=[pltpu.VMEM((TR_FG, LANES), jnp.float32)],
        compiler_params=pltpu.CompilerParams(
            dimension_semantics=("arbitrary", "arbitrary"), vmem_limit_bytes=VMEM_LIMIT_BYTES),
        name="proj",
    )(x, g1, wnat, wtr, bft, cn, s1n, s2n, ct, st, tri, en, pt, onesn)


def _sweep_tables(S, chunk):
    nq, ratio = S // ATT_TQ, ATT_TK // ATT_TQ
    diag = [(i, i // ratio, i % ratio) for i in range(nq)]
    full = [(i, j, 0) for i in range(nq) for j in range(i // ratio)]
    assert len(full) % ATT_GROUP == 0 and len(diag) % ATT_GROUP == 0
    kk = np.arange(ATT_TK)[:, None] // chunk
    masks = []
    for r in range(ratio):
        qq = (r * ATT_TQ + np.arange(ATT_TQ))[None, :] // chunk
        masks.append(np.where(kk <= qq, 0.0, NEG).astype(np.float32))
    return np.array(full + diag, np.int32).T.copy(), len(full), np.stack(masks)


def _init_sweep_state(m_s, acc_s):
    m_s[...] = jnp.full_like(m_s, NEG)
    acc_s[...] = jnp.zeros_like(acc_s)


def _with_ones_rows(v_t):
    return jnp.concatenate([v_t, jnp.ones((ATT_ONES_ROWS, v_t.shape[1]), v_t.dtype)], axis=0)


def _flat_sweep(tab_ref, mask_ref, n_plain, n_heads, scores, values, m_s, acc_s, sbuf, xbuf, pbuf, abuf):
    n_groups, n_plain_groups = tab_ref.shape[1] // ATT_GROUP, n_plain // ATT_GROUP
    assert 2 <= n_plain_groups <= n_groups and ATT_UNROLL % ATT_SLOTS == 0

    def stage_scores(u, par, masked, pairs=range(ATT_GROUP)):
        for a in pairs:
            t = u * ATT_GROUP + a
            it, jt = tab_ref[0, t], tab_ref[1, t]
            for h in range(n_heads):
                s = scores(h, it, jt)
                if masked:
                    s = s + mask_ref[tab_ref[2, t]]
                sbuf[par][a, h] = s
                xbuf[par][a, h] = jnp.broadcast_to(jnp.max(s, axis=0, keepdims=True), (SUBLANES, ATT_TQ))

    def stage_softmax(u, par):
        for a in range(ATT_GROUP):
            it = tab_ref[0, u * ATT_GROUP + a]
            for h in range(n_heads):
                m_old = m_s[h, it]
                m_new = jnp.maximum(m_old, xbuf[par][a, h])
                alpha = jnp.exp2(m_old - m_new)
                p = jnp.exp2(sbuf[par][a, h] - m_new[0:1])
                m_s[h, it] = m_new
                pbuf[par][a, h] = p.astype(jnp.bfloat16)
                abuf[par][a, h] = alpha

    def stage_values(u, par, pairs=range(ATT_GROUP)):
        for a in pairs:
            t = u * ATT_GROUP + a
            it, jt = tab_ref[0, t], tab_ref[1, t]
            for h in range(n_heads):
                pv = jnp.dot(_with_ones_rows(values(h, jt)), pbuf[par][a, h],
                             preferred_element_type=jnp.float32)
                acc_s[h, it] = abuf[par][a, h][0:1] * acc_s[h, it] + pv

    def step(u, par, masked, with_values=True):
        for a0 in range(0, ATT_GROUP, ATT_BURST):
            pairs = range(a0, min(a0 + ATT_BURST, ATT_GROUP))
            stage_scores(u, par, masked, pairs)
            if with_values:
                stage_values(u - 2, (par - 2) % ATT_SLOTS, pairs)
        stage_softmax(u - 1, (par - 1) % ATT_SLOTS)

    def run(lo, hi, masked):
        first = lo + (hi - lo) % ATT_UNROLL
        for u in range(lo, first):
            step(u, u % ATT_SLOTS, masked)

        def body(v, c):
            for k in range(ATT_UNROLL):
                step(first + ATT_UNROLL * v + k, (first + k) % ATT_SLOTS, masked)
            return c

        lax.fori_loop(0, (hi - first) // ATT_UNROLL, body, 0)

    stage_scores(0, 0, False)
    step(1, 1, False, with_values=False)
    run(2, n_plain_groups, False)
    run(n_plain_groups, n_groups, True)
    stage_values(n_groups - 2, (n_groups - 2) % ATT_SLOTS)
    stage_softmax(n_groups - 1, (n_groups - 1) % ATT_SLOTS)
    stage_values(n_groups - 1, (n_groups - 1) % ATT_SLOTS)


def _sweep_scratch(n_heads, nq, dv):
    f32 = jnp.float32
    tile = (ATT_GROUP, n_heads, ATT_TK, ATT_TQ)
    row = (ATT_GROUP, n_heads, SUBLANES, ATT_TQ)
    return ([
        pltpu.VMEM((n_heads, nq, SUBLANES, ATT_TQ), f32),
        pltpu.VMEM((n_heads, nq, dv + ATT_ONES_ROWS, ATT_TQ), f32),
    ] + [pltpu.VMEM(tile, f32)] * ATT_SLOTS
      + [pltpu.VMEM(tile, jnp.bfloat16)] * ATT_SLOTS
      + [pltpu.VMEM(row, f32)] * (2 * ATT_SLOTS))


def _for_each_query_block(nq, fn):
    def body(g, c):
        for k in range(ATT_FIN_UNROLL):
            fn(g * ATT_FIN_UNROLL + k, c)
        return c

    lax.fori_loop(0, nq // ATT_FIN_UNROLL, body, 0)


def _key_slice(jt):
    return pl.ds(pl.multiple_of(jt * ATT_TK, ATT_TK), ATT_TK)


def _query_slice(it):
    return pl.ds(pl.multiple_of(it * ATT_TQ, ATT_TQ), ATT_TQ)


def _dattn_kernel(n_plain, tab_ref, q_ref, k_ref, v_ref, mask_ref, lq1_ref, lk1_ref, lq2_ref, lk2_ref, g_ref,
                  o_ref, m_s, acc_s, *staging):
    nq = m_s.shape[1]
    sbuf, pbuf, xbuf, abuf = (staging[n * ATT_SLOTS:(n + 1) * ATT_SLOTS] for n in range(4))
    row = lax.broadcasted_iota(jnp.int32, (2 * DIFF_DH, ATT_TQ), 0)

    def scores(h, it, jt):
        q = q_ref[0, 0, it]
        keep = (row < DIFF_DH) if h == 0 else (row >= DIFF_DH)
        rhs = jnp.where(keep, q, jnp.zeros_like(q))
        return jnp.dot(k_ref[0, 0, _key_slice(jt), :], rhs, preferred_element_type=jnp.float32)

    def values(h, jt):
        return v_ref[0, 0, jt]

    _init_sweep_state(m_s, acc_s)
    _flat_sweep(tab_ref, mask_ref, n_plain, 2, scores, values, m_s, acc_s, sbuf, xbuf, pbuf, abuf)

    lam = (jnp.exp(jnp.sum(lq1_ref[...] * lk1_ref[...], axis=-1, keepdims=True))
           - jnp.exp(jnp.sum(lq2_ref[...] * lk2_ref[...], axis=-1, keepdims=True)) + LAM_INIT)
    gain = g_ref[...] * (1.0 - LAM_INIT)

    def finalize(it, c):
        a1, a2 = acc_s[0, it], acc_s[1, it]
        o = (a1[:DIFF_VD] / a1[DIFF_VD:DIFF_VD + 1]
             - lam * (a2[:DIFF_VD] / a2[DIFF_VD:DIFF_VD + 1]))
        ms = jnp.mean(o * o, axis=0, keepdims=True)
        o = o * lax.rsqrt(ms + EPS) * gain
        o_ref[0, _query_slice(it), :] = o.T.astype(o_ref.dtype)
        return c

    _for_each_query_block(nq, finalize)


def _dattn_call(tab, n_plain, masks, qd, kd, vd, lq1, lk1, lq2, lk2, gcol):
    B, _, nq = qd.shape[:3]
    S = nq * ATT_TQ
    vec = lambda: pl.BlockSpec((1, DIFF_DH), lambda b, h, *_: (0, 0))
    return pl.pallas_call(
        functools.partial(_dattn_kernel, n_plain),
        grid_spec=pltpu.PrefetchScalarGridSpec(
            num_scalar_prefetch=1,
            grid=(B, DIFF_HEADS),
            in_specs=[
                pl.BlockSpec((1, 1) + qd.shape[2:], lambda b, h, *_: (b, h, 0, 0, 0)),
                pl.BlockSpec((1, 1, S, LANES), lambda b, h, *_: (b, h, 0, 0)),
                pl.BlockSpec((1, 1) + vd.shape[2:], lambda b, h, *_: (b, h, 0, 0, 0)),
                pl.BlockSpec(masks.shape, lambda b, h, *_: (0, 0, 0)),
                vec(), vec(), vec(), vec(),
                pl.BlockSpec((DIFF_VD, ATT_TQ), lambda b, h, *_: (0, 0)),
            ],
            out_specs=pl.BlockSpec((1, S, DIFF_VD), lambda b, h, *_: (b, 0, h)),
            scratch_shapes=_sweep_scratch(2, nq, DIFF_VD),
        ),
        out_shape=jax.ShapeDtypeStruct((B, S, DIFF_HEADS * DIFF_VD), jnp.bfloat16),
        compiler_params=pltpu.CompilerParams(
            dimension_semantics=("arbitrary", "arbitrary"), vmem_limit_bytes=VMEM_LIMIT_BYTES),
        name="dattn",
    )(tab, qd, kd, vd, masks, lq1, lk1, lq2, lk2, gcol)


def _fattn_kernel(n_plain, tab_ref, q_ref, k_ref, v_ref, mask_ref, o_ref, m_s, acc_s, *staging):
    nq = m_s.shape[1]
    sbuf, pbuf, xbuf, abuf = (staging[n * ATT_SLOTS:(n + 1) * ATT_SLOTS] for n in range(4))

    def scores(h, it, jt):
        q = q_ref[0, h, it]
        return jnp.dot(k_ref[0, h, _key_slice(jt), :], q, preferred_element_type=jnp.float32)

    def values(h, jt):
        return v_ref[0, h, jt]

    _init_sweep_state(m_s, acc_s)
    _flat_sweep(tab_ref, mask_ref, n_plain, 2, scores, values, m_s, acc_s, sbuf, xbuf, pbuf, abuf)

    def finalize(it, c):
        heads = [acc_s[h, it] for h in range(2)]
        o = jnp.concatenate([a[:FOX_DH] / a[FOX_DH:FOX_DH + 1] for a in heads], axis=0)
        o_ref[0, _query_slice(it), :] = o.T.astype(o_ref.dtype)
        return c

    _for_each_query_block(nq, finalize)


def _fattn_call(tab, n_plain, masks, qf, kf, vf):
    B, _, nq = qf.shape[:3]
    S = nq * ATT_TQ
    return pl.pallas_call(
        functools.partial(_fattn_kernel, n_plain),
        grid_spec=pltpu.PrefetchScalarGridSpec(
            num_scalar_prefetch=1,
            grid=(B, FOX_HEADS // 2),
            in_specs=[
                pl.BlockSpec((1, 2) + qf.shape[2:], lambda b, p, *_: (b, p, 0, 0, 0)),
                pl.BlockSpec((1, 2, S, LANES), lambda b, p, *_: (b, p, 0, 0)),
                pl.BlockSpec((1, 2) + vf.shape[2:], lambda b, p, *_: (b, p, 0, 0, 0)),
                pl.BlockSpec(masks.shape, lambda b, p, *_: (0, 0, 0)),
            ],
            out_specs=pl.BlockSpec((1, S, 2 * FOX_DH), lambda b, p, *_: (b, 0, p)),
            scratch_shapes=_sweep_scratch(2, nq, FOX_DH),
        ),
        out_shape=jax.ShapeDtypeStruct((B, S, FOX_HEADS * FOX_DH), jnp.bfloat16),
        compiler_params=pltpu.CompilerParams(
            dimension_semantics=("arbitrary", "arbitrary"), vmem_limit_bytes=VMEM_LIMIT_BYTES),
        name="fattn",
    )(tab, qf, kf, vf, masks)


def _ffn_kernel(x_ref, md_ref, mf_ref, wo_ref, g2_ref, wg_ref, wu_ref, wd_ref, gf_ref, o_ref):
    half = md_ref.shape[1]
    x1 = (x_ref[...]
          + jnp.dot(md_ref[...], wo_ref[:half], preferred_element_type=jnp.float32)
          + jnp.dot(mf_ref[...], wo_ref[half:], preferred_element_type=jnp.float32))
    ms = jnp.mean(x1 * x1, axis=-1, keepdims=True)
    h2 = (x1 * lax.rsqrt(ms + EPS) * g2_ref[...]).astype(jnp.bfloat16)
    g = jnp.dot(h2, wg_ref[...], preferred_element_type=jnp.float32)
    u = jnp.dot(h2, wu_ref[...], preferred_element_type=jnp.float32)
    act = (g * jax.nn.sigmoid(g) * u).astype(jnp.bfloat16)
    y = x1 + jnp.dot(act, wd_ref[...], preferred_element_type=jnp.float32)
    ms = jnp.mean(y * y, axis=-1, keepdims=True)
    o_ref[...] = y * lax.rsqrt(ms + EPS) * gf_ref[...]


def _ffn_call(x2, md, mf, wo, g2, wg, wu, wd, gf):
    n, D = x2.shape
    tm = FFN_TM
    half = md.shape[1]
    resident = lambda shape: pl.BlockSpec(shape, lambda i: (0, 0), pipeline_mode=pl.Buffered(1))
    return pl.pallas_call(
        _ffn_kernel,
        grid=(n // tm,),
        in_specs=[
            pl.BlockSpec((tm, D), lambda i: (i, 0)),
            pl.BlockSpec((tm, half), lambda i: (i, 0)),
            pl.BlockSpec((tm, half), lambda i: (i, 0)),
            resident((2 * half, D)),
            resident((1, D)),
            resident((D, D_FF)),
            resident((D, D_FF)),
            resident((D_FF, D)),
            resident((1, D)),
        ],
        out_specs=pl.BlockSpec((tm, D), lambda i: (i, 0)),
        out_shape=jax.ShapeDtypeStruct((n, D), jnp.float32),
        compiler_params=pltpu.CompilerParams(
            dimension_semantics=("arbitrary",), vmem_limit_bytes=VMEM_LIMIT_BYTES),
        name="ffn",
    )(x2, md, mf, wo, g2, wg, wu, wd, gf)


def _placement_tables():
    en = np.zeros((LANES, KF_W), np.float32)
    pt = np.zeros((FOX_HEADS * AUG_ROWS, 4 * TR_FG), np.float32)
    onesn = np.zeros((1, KF_W), np.float32)
    for h in range(FOX_HEADS):
        for p in range(3):
            en[p * FOX_HEADS + h, h * LANES + AUG_BIAS + p] = 1.0
            onesn[0, h * LANES + AUG_ONES + p] = 1.0
            pt[h * AUG_ROWS + (AUG_ONES - FOX_DH) + p, p * TR_FG + h] = 1.0
            pt[h * AUG_ROWS + (AUG_BIAS - FOX_DH) + p, 3 * TR_FG] = 1.0
    tri = np.tril(np.ones((PROJ_TM, PROJ_TM), np.float32))
    return en, pt, onesn, tri


def _rope_tables(S):
    inv_freq = ROPE_THETA ** (-np.arange(0, ROPE_DIM, 2, dtype=np.float64) / ROPE_DIM)
    ang = np.arange(S, dtype=np.float64)[:, None] * inv_freq[None, :]
    cos, sin = np.cos(ang), np.sin(ang)
    lane = np.arange(LANES) % DIFF_DH
    sel = lane % ROPE_HALF
    lo = (lane < ROPE_HALF)[None, :]
    hi = ((lane >= ROPE_HALF) & (lane < ROPE_DIM))[None, :]
    cos_l, sin_l = cos[:, sel], sin[:, sel]
    cn = np.where(lo | hi, cos_l, 1.0)
    s1n = np.where(lo, -sin_l, 0.0)
    s2n = np.where(hi, sin_l, 0.0)
    return [jnp.asarray(t, jnp.float32) for t in (cn, s1n, s2n, cos.T, sin.T)]


def kernel(x, norm1_g, w_in, b_f, lam_q1, lam_k1, lam_q2, lam_k2, subln_g, w_out, norm2_g, w_gate, w_up,
           w_down, normf_g):
    f32, bf16 = jnp.float32, jnp.bfloat16
    B, S, D = x.shape
    en, pt, onesn, tri = _placement_tables()
    cn, s1n, s2n, ct, st = _rope_tables(S)

    w = w_in[0]
    o_dq, o_dk, o_dv, o_fq, o_fk, o_fv, o_fg = 0, 512, 1024, 1536, 2048, 2560, 3072
    wnat = jnp.concatenate([w[:, o_dk:o_dv], w[:, o_fk:o_fv]], axis=1).astype(bf16)
    wtr = jnp.concatenate(
        [w[:, o_dq:o_dk], w[:, o_dv:o_fq], w[:, o_fq:o_fk], w[:, o_fv:o_fg],
         jnp.pad(w[:, o_fg:], ((0, 0), (0, TR_FG - FOX_HEADS)))], axis=1).T.astype(bf16)
    bft = jnp.broadcast_to(jnp.pad(b_f[0].astype(f32), (0, TR_FG - FOX_HEADS))[:, None], (TR_FG, PROJ_TM))

    kd, kf, qd, vd, qf, vf = _proj_call(
        x, norm1_g[0][None, :], wnat, wtr, bft, cn, s1n, s2n, ct, st,
        jnp.asarray(tri, bf16), jnp.asarray(en, bf16), jnp.asarray(pt, bf16), jnp.asarray(onesn, f32))

    gcol = jnp.broadcast_to(subln_g[0].astype(f32)[:, None], (DIFF_VD, ATT_TQ))
    row = lambda v: v[0].astype(f32)[None, :]
    tab, n_plain, masks_d = _sweep_tables(S, CHUNK)
    _, _, masks_f = _sweep_tables(S, 1)
    tab = jnp.asarray(tab)
    mixd = _dattn_call(tab, n_plain, jnp.asarray(masks_d), qd, kd, vd,
                       row(lam_q1), row(lam_k1), row(lam_q2), row(lam_k2), gcol)
    mixf = _fattn_call(tab, n_plain, jnp.asarray(masks_f), qf, kf, vf)

    half = DIFF_HEADS * DIFF_VD
    out = _ffn_call(
        x.reshape(B * S, D), mixd.reshape(B * S, half), mixf.reshape(B * S, half),
        w_out[0].astype(bf16), norm2_g[0][None, :], w_gate[0].astype(bf16), w_up[0].astype(bf16),
        w_down[0].astype(bf16), normf_g[None, :])
    return out.reshape(B, S, D)
```
